```python
import math
import jax, jax.numpy as jnp
from jax import lax
import numpy as np

D_MODEL = 1024
BATCH = 16
SEQ = 2048
DEPTH = 2

RMS_EPS = 1e-6
MASK_VALUE = -1e30
LB_FLOOR = 1e-20
ATT_DK = 64
ATT_HEADS = D_MODEL // (2 * ATT_DK)
ATT_DV = 2 * ATT_DK
ATT_QK_WIDTH = ATT_HEADS * 2 * ATT_DK
ATT_V_WIDTH = ATT_HEADS * ATT_DV
ATT_BLOCK = 128
POOL_WINDOWS = (2, 4, 8, 16)
POOL_GROUPS = 4
POOL_WIDTH = D_MODEL
POOL_GW = POOL_WIDTH // POOL_GROUPS
HGRN_DK = 128
HGRN_DV = 128
HGRN_HEADS = D_MODEL // HGRN_DV
HGRN_K_WIDTH = HGRN_HEADS * HGRN_DK
HGRN_V_WIDTH = HGRN_HEADS * HGRN_DV
HGRN_CHUNK = 64
N_BRANCHES = 3
IN_SIZES = (ATT_QK_WIDTH, ATT_QK_WIDTH, ATT_V_WIDTH, POOL_WIDTH,
            HGRN_K_WIDTH, HGRN_K_WIDTH, HGRN_V_WIDTH, HGRN_V_WIDTH, N_BRANCHES * D_MODEL)
IN_COLS = sum(IN_SIZES)
PEER_HEADS = 8
PEER_DQ = 256
PEER_DHALF = PEER_DQ // 2
PEER_NKEYS = 128
PEER_NEXPERTS = PEER_NKEYS * PEER_NKEYS
PEER_TOPK = 16
PEER_TOKEN_BLOCK = 128

kernel_name = "hybrid_diffattn_pool_hgrn2_peer_adaln"


def rms_norm(x, g):
    xf = x.astype(jnp.float32)
    y = xf * lax.rsqrt(jnp.mean(xf * xf, axis=-1, keepdims=True) + RMS_EPS)
    return (y * g.astype(jnp.float32)).astype(x.dtype)


def lambda_init_fn(layer):
    return 0.8 - 0.6 * math.exp(-0.3 * layer)


def alibi_slopes():
    return jnp.asarray(np.array([2.0 ** (-8.0 * (h + 1) / ATT_HEADS) for h in range(ATT_HEADS)], dtype=np.float32))


def diff_attention(q, k, v, lam_qk, subln_g, layer):
    B, S, _ = q.shape
    lam_init = lambda_init_fn(layer)
    lq = lam_qk.astype(jnp.float32)
    lam = jnp.exp(jnp.sum(lq[0] * lq[1])) - jnp.exp(jnp.sum(lq[2] * lq[3])) + lam_init
    qh = q.reshape(B, S, ATT_HEADS, 2, ATT_DK).transpose(0, 3, 2, 1, 4)
    kh = k.reshape(B, S, ATT_HEADS, 2, ATT_DK).transpose(0, 3, 2, 1, 4)
    vh = v.reshape(B, S, ATT_HEADS, ATT_DV)
    slopes = alibi_slopes()[:, None, None]
    pos = jnp.arange(S)
    scale = ATT_DK ** -0.5
    outs = []
    for blk in range(S // ATT_BLOCK):
        q0 = blk * ATT_BLOCK
        kv_len = q0 + ATT_BLOCK
        qb = qh[:, :, :, q0:kv_len]
        kb = kh[:, :, :, :kv_len]
        vb = vh[:, :kv_len]
        s = jnp.einsum('bmhqd,bmhkd->bmhqk', qb, kb).astype(jnp.float32) * scale
        dist = (pos[q0:kv_len, None] - pos[None, :kv_len]).astype(jnp.float32)
        bias = jnp.where(dist[None] >= 0, -slopes * dist[None], MASK_VALUE)
        p = jax.nn.softmax(s + bias, axis=-1)
        a = p[:, 0] - lam * p[:, 1]
        outs.append(jnp.einsum('bhqk,bkhd->bqhd', a.astype(vb.dtype), vb))
    o = jnp.concatenate(outs, axis=1)
    o = rms_norm(o, subln_g) * (1.0 - lam_init)
    return o.reshape(B, S, ATT_V_WIDTH)


def multiscale_pool(p, pool_w, pool_scale):
    B, S, _ = p.shape
    cs = jnp.cumsum(p.astype(jnp.float32), axis=1)
    t1 = jnp.arange(1, S + 1)
    means = []
    for g, w in enumerate(POOL_WINDOWS):
        csg = cs[..., g * POOL_GW:(g + 1) * POOL_GW]
        shifted = jnp.pad(csg, ((0, 0), (w, 0), (0, 0)))[:, :S]
        count = jnp.minimum(t1, w).astype(jnp.float32)
        means.append((csg - shifted) / count[None, :, None])
    pooled = jnp.concatenate(means, axis=-1).astype(p.dtype) - p
    y = jnp.einsum('bsgc,gcd->bsgd', pooled.reshape(B, S, POOL_GROUPS, POOL_GW), pool_w)
    return y.reshape(B, S, POOL_WIDTH) * pool_scale


def hgrn2(q, f, i, g, lb, norm_g):
    B, S, _ = q.shape
    nc = S // HGRN_CHUNK

    def heads(t, d):
        return t.reshape(B, nc, HGRN_CHUNK, HGRN_HEADS, d).transpose(1, 0, 3, 2, 4).astype(jnp.float32)

    qh = heads(jax.nn.silu(q), HGRN_DK)
    lbh = lb.astype(jnp.float32).reshape(HGRN_HEADS, 1, HGRN_DK)
    logf = jnp.logaddexp(jnp.log(jnp.maximum(lbh, LB_FLOOR)),
                         jnp.log1p(-lbh) + jax.nn.log_sigmoid(heads(f, HGRN_DK)))
    kh = -jnp.expm1(logf)
    vh = heads(i, HGRN_DV)
    mask = jnp.tril(jnp.ones((HGRN_CHUNK, HGRN_CHUNK), dtype=bool))[:, :, None]

    def step(state, inp):
        qc, kc, lfc, vc = inp
        b = jnp.cumsum(lfc, axis=-2)
        diff = b[:, :, :, None, :] - b[:, :, None, :, :]
        decay = jnp.exp(jnp.where(mask, diff, MASK_VALUE))
        attn = jnp.einsum('bhtd,bhtsd,bhsd->bhts', qc, decay, kc)
        b_last = b[:, :, -1:, :]
        o = (jnp.einsum('bhts,bhsv->bhtv', attn, vc)
             + jnp.einsum('bhtd,bhdv->bhtv', qc * jnp.exp(b), state))
        state = (jnp.exp(b_last)[:, :, 0, :, None] * state
                 + jnp.einsum('bhsd,bhsv->bhdv', kc * jnp.exp(b_last - b), vc))
        return state, o

    s0 = jnp.zeros((B, HGRN_HEADS, HGRN_DK, HGRN_DV), jnp.float32)
    _, o = lax.scan(step, s0, (qh, kh, logf, vh))
    o = o.transpose(1, 0, 3, 2, 4).reshape(B, S, HGRN_HEADS, HGRN_DV).astype(q.dtype)
    o = rms_norm(o, norm_g) * jax.nn.silu(g.reshape(B, S, HGRN_HEADS, HGRN_DV))
    return o.reshape(B, S, HGRN_V_WIDTH)


def hybrid_mixer(h, layer, w_in, lam_qk, subln_g, pool_w, pool_scale, lb, hgrn_g, w_out):
    B, S, _ = h.shape
    z = h @ w_in
    points = []
    acc = 0
    for sz in IN_SIZES[:-1]:
        acc += sz
        points.append(acc)
    aq, ak, av, pp, hq, hf, hi, hg, zg = jnp.split(z, points, axis=-1)
    a_out = diff_attention(aq, ak, av, lam_qk, subln_g, layer)
    p_out = multiscale_pool(pp, pool_w, pool_scale)
    r_out = hgrn2(hq, hf, hi, hg, lb, hgrn_g)
    gates = jax.nn.sigmoid(zg).reshape(B, S, N_BRANCHES, D_MODEL)
    merged = gates[:, :, 0] * a_out + gates[:, :, 1] * p_out + gates[:, :, 2] * r_out
    return merged @ w_out


def peer_ffn(h, w_query, sub_keys, u_tab, v_tab):
    B, S, D = h.shape
    T = B * S
    xt = h.reshape(T, D)
    q = (xt @ w_query).reshape(T, PEER_HEADS, 2, PEER_DHALF)
    s = jnp.einsum('thpd,pkd->thpk', q, sub_keys).astype(jnp.float32)
    s_a, i_a = lax.top_k(s[:, :, 0], PEER_TOPK)
    s_b, i_b = lax.top_k(s[:, :, 1], PEER_TOPK)
    n_cand = PEER_TOPK * PEER_TOPK
    cand_s = (s_a[..., :, None] + s_b[..., None, :]).reshape(T, PEER_HEADS, n_cand)
    cand_i = (i_a[..., :, None] * PEER_NKEYS + i_b[..., None, :]).reshape(T, PEER_HEADS, n_cand)
    top_s, pos = lax.top_k(cand_s, PEER_TOPK)
    idx = jnp.take_along_axis(cand_i, pos, axis=-1)
    gate = jax.nn.softmax(top_s, axis=-1).astype(h.dtype)
    nb = T // PEER_TOKEN_BLOCK

    def apply(args):
        xb, ib, gb = args
        u = u_tab[ib]
        act = jax.nn.gelu(jnp.einsum('thkd,td->thk', u, xb), approximate=False)
        return jnp.einsum('thk,thkd->td', gb * act, v_tab[ib])

    out = lax.map(apply, (xt.reshape(nb, PEER_TOKEN_BLOCK, D),
                          idx.reshape(nb, PEER_TOKEN_BLOCK, PEER_HEADS, PEER_TOPK),
                          gate.reshape(nb, PEER_TOKEN_BLOCK, PEER_HEADS, PEER_TOPK)))
    return out.reshape(B, S, D)


def setup_inputs(seed: int = 0) -> dict:
    key = jax.random.key(seed)
    ks = jax.random.split(key, 19)
    D = D_MODEL

    def nrm(k, shape, scale):
        return jax.random.normal(k, shape, jnp.float32) * scale

    return {
        "x": nrm(ks[0], (BATCH, SEQ, D), 1.0),
        "c": nrm(ks[1], (BATCH, D), 1.0),
        "ada_w": nrm(ks[2], (DEPTH, D, 6 * D), 0.5 * D ** -0.5),
        "ada_b": nrm(ks[3], (DEPTH, 6 * D), 0.02),
        "norm_mix_g": 1.0 + nrm(ks[4], (DEPTH, D), 0.05),
        "w_in": nrm(ks[5], (DEPTH, D, IN_COLS), D ** -0.5),
        "diff_lambda": nrm(ks[6], (DEPTH, 4, ATT_DK), 0.1),
        "diff_subln_g": 1.0 + nrm(ks[7], (DEPTH, ATT_DV), 0.05),
        "pool_w": nrm(ks[8], (DEPTH, POOL_GROUPS, POOL_GW, POOL_GW), POOL_GW ** -0.5),
        "pool_scale": 1.0 + nrm(ks[9], (DEPTH, POOL_WIDTH), 0.05),
        "hgrn_lower_bounds": 1.0 + nrm(ks[10], (DEPTH, HGRN_K_WIDTH), 0.1),
        "hgrn_norm_g": 1.0 + nrm(ks[11], (DEPTH, HGRN_DV), 0.05),
        "w_out": nrm(ks[12], (DEPTH, D, D), D ** -0.5),
        "norm_ffn_g": 1.0 + nrm(ks[13], (DEPTH, D), 0.05),
        "peer_w_query": nrm(ks[14], (DEPTH, D, PEER_HEADS * PEER_DQ), D ** -0.5),
        "peer_sub_keys": nrm(ks[15], (DEPTH, 2, PEER_NKEYS, PEER_DHALF), PEER_DHALF ** -0.5),
        "peer_u": nrm(ks[16], (DEPTH, PEER_NEXPERTS, D), D ** -0.5),
        "peer_v": nrm(ks[17], (DEPTH, PEER_NEXPERTS, D), 1.0),
        "final_g": 1.0 + nrm(ks[18], (D,), 0.05),
    }


def reference(x, c, ada_w, ada_b, norm_mix_g, w_in, diff_lambda, diff_subln_g, pool_w, pool_scale,
              hgrn_lower_bounds, hgrn_norm_g, w_out, norm_ffn_g, peer_w_query, peer_sub_keys,
              peer_u, peer_v, final_g):
    p_lb = jax.nn.softmax(hgrn_lower_bounds.astype(jnp.float32), axis=0)
    lb_all = jnp.cumsum(p_lb, axis=0) - p_lb[0:1]
    c_act = jax.nn.silu(c)
    for l in range(DEPTH):
        ada = (c_act @ ada_w[l] + ada_b[l])[:, None, :]
        sh1, sc1, g1, sh2, sc2, g2 = jnp.split(ada, 6, axis=-1)
        h = rms_norm(x, norm_mix_g[l]) * (1.0 + sc1) + sh1
        x = x + g1 * hybrid_mixer(h, l, w_in[l], diff_lambda[l], diff_subln_g[l], pool_w[l],
                                  pool_scale[l], lb_all[l], hgrn_norm_g[l], w_out[l])
        h = rms_norm(x, norm_ffn_g[l]) * (1.0 + sc2) + sh2
        x = x + g2 * peer_ffn(h, peer_w_query[l], peer_sub_keys[l], peer_u[l], peer_v[l])
    return rms_norm(x, final_g)
```

```python
import functools
import math

import jax
import jax.numpy as jnp
from jax import lax
from jax.experimental import pallas as pl
from jax.experimental.pallas import tpu as pltpu

F32 = jnp.float32
BF16 = jnp.bfloat16

RMS_EPS = 1e-6
MASK_VALUE = -1e30
LB_FLOOR = 1e-20
LANES = 128

ATT_DK = 64
ATT_DV = 2 * ATT_DK
POOL_WINDOWS = (2, 4, 8, 16)
HGRN_D = 128
HGRN_CHUNK = 64
HGRN_SUB = 16
PEER_HEADS = 8
PEER_NKEYS = 128
PEER_TOPK = 16
PEER_DHALF = 128

VMEM_LIMIT = 56 * 1024 * 1024


def _cparams(sem):
    return pltpu.CompilerParams(dimension_semantics=sem, vmem_limit_bytes=VMEM_LIMIT)


def _dot(a, b, **kw):
    return jnp.dot(a, b, preferred_element_type=F32, **kw)


def _dot_nt(a, b):
    return lax.dot_general(a, b, (((1,), (1,)), ((), ())), preferred_element_type=F32)


def _dot_tn(a, b):
    return lax.dot_general(a, b, (((0,), (0,)), ((), ())), preferred_element_type=F32)


def _rms(x, g):
    return x * lax.rsqrt(jnp.mean(x * x, axis=-1, keepdims=True) + RMS_EPS) * g


def _silu(x):
    return x * jax.nn.sigmoid(x)


def _ada_kernel(c_ref, w_ref, b_ref, o_ref):
    o_ref[0] = _dot(_silu(c_ref[...]), w_ref[0], precision=lax.Precision.HIGHEST) + b_ref[0]


def _ada(c, ada_w, ada_b):
    depth, d, n = ada_w.shape
    bsz = c.shape[0]
    tn = 1024
    return pl.pallas_call(
        _ada_kernel,
        grid=(depth, n // tn),
        in_specs=[pl.BlockSpec((bsz, d), lambda l, j: (0, 0)),
                  pl.BlockSpec((1, d, tn), lambda l, j: (l, 0, j)),
                  pl.BlockSpec((1, 1, tn), lambda l, j: (l, 0, j))],
        out_specs=pl.BlockSpec((1, bsz, tn), lambda l, j: (l, 0, j)),
        out_shape=jax.ShapeDtypeStruct((depth, bsz, n), F32),
        compiler_params=_cparams(("arbitrary", "arbitrary")),
        name="ada_proj",
    )(c, ada_w, ada_b.reshape(depth, 1, n))


def _norm_mod_matmul_kernel(x_ref, g_ref, sc_ref, sh_ref, w_ref, o_ref, h_scr):
    @pl.when(pl.program_id(1) == 0)
    def _():
        h = _rms(x_ref[...], g_ref[...]) * (1.0 + sc_ref[0]) + sh_ref[0]
        h_scr[...] = h.astype(BF16)

    o_ref[...] = _dot(h_scr[...], w_ref[...]).astype(o_ref.dtype)


def _norm_mod_matmul(x2, g, sc, sh, w, out_dtype, seq):
    t, d = x2.shape
    n = w.shape[1]
    tm = min(1024, seq)
    tn = 1024
    return pl.pallas_call(
        _norm_mod_matmul_kernel,
        grid=(t // tm, n // tn),
        in_specs=[pl.BlockSpec((tm, d), lambda i, j: (i, 0)),
                  pl.BlockSpec((1, d), lambda i, j: (0, 0)),
                  pl.BlockSpec((1, 1, d), lambda i, j: ((i * tm) // seq, 0, 0)),
                  pl.BlockSpec((1, 1, d), lambda i, j: ((i * tm) // seq, 0, 0)),
                  pl.BlockSpec((d, tn), lambda i, j: (0, j))],
        out_specs=pl.BlockSpec((tm, tn), lambda i, j: (i, j)),
        out_shape=jax.ShapeDtypeStruct((t, n), out_dtype),
        scratch_shapes=[pltpu.VMEM((tm, d), BF16)],
        compiler_params=_cparams(("arbitrary", "arbitrary")),
        name="norm_mod_matmul",
    )(x2, g.reshape(1, d), sc, sh, w)


def _attn_kernel(lam_ref, g_ref, slope_ref, q_ref, k_ref, v_ref, o_ref, m_scr, l_scr, acc_scr,
                 *, lam_init, tq):
    qi = pl.program_id(2)
    slope = slope_ref[0]
    lane = lax.broadcasted_iota(jnp.int32, (1, 2 * ATT_DK), 1)
    q = q_ref[...] * (ATT_DK ** -0.5)
    zero = jnp.zeros_like(q)
    qm = (jnp.where(lane < ATT_DK, q, zero), jnp.where(lane >= ATT_DK, q, zero))
    col = lax.broadcasted_iota(jnp.int32, (1, tq), 1)

    m_scr[...] = jnp.full(m_scr.shape, MASK_VALUE, F32)
    l_scr[...] = jnp.zeros(l_scr.shape, F32)
    acc_scr[...] = jnp.zeros(acc_scr.shape, F32)

    def step(j, masked):
        k = k_ref[pl.ds(pl.multiple_of(j * tq, tq), tq), :]
        v = v_ref[pl.ds(pl.multiple_of(j * tq, tq), tq), :]
        bias = slope * (col + (j - qi) * tq).astype(F32)
        if masked:
            row = lax.broadcasted_iota(jnp.int32, (tq, tq), 0)
            keep = row >= lax.broadcasted_iota(jnp.int32, (tq, tq), 1)
        for m in range(2):
            s = _dot_nt(qm[m], k) + bias
            if masked:
                s = jnp.where(keep, s, MASK_VALUE)
            m_old = m_scr[m]
            m_new = jnp.maximum(m_old, jnp.max(s, axis=1, keepdims=True))
            p = jnp.exp(s - m_new)
            alpha = jnp.exp(m_old - m_new)
            l_scr[m] = alpha * l_scr[m] + jnp.sum(p, axis=1, keepdims=True)
            acc_scr[m] = alpha * acc_scr[m] + _dot(p.astype(BF16), v)
            m_scr[m] = m_new

    def body(j, carry):
        step(j, False)
        return carry

    lax.fori_loop(0, qi, body, 0)
    step(qi, True)

    lq = lam_ref[...]
    lam = (jnp.exp(jnp.sum(lq[0:1] * lq[1:2], axis=1, keepdims=True))
           - jnp.exp(jnp.sum(lq[2:3] * lq[3:4], axis=1, keepdims=True)) + lam_init)
    o = acc_scr[0] / l_scr[0] - lam * (acc_scr[1] / l_scr[1])
    o_ref[...] = _rms(o, g_ref[...]) * (1.0 - lam_init)


def _diff_attention(z_att, lam_qk, subln_g, layer, bsz, seq):
    t = z_att.shape[0]
    d = z_att.shape[1] // 3
    heads = d // ATT_DV
    tq = min(256, seq)
    nq = seq // tq
    lam_init = 0.8 - 0.6 * math.exp(-0.3 * layer)
    slopes = jnp.asarray([2.0 ** (-8.0 * (h + 1) / heads) for h in range(heads)], F32)
    slopes = jnp.broadcast_to(slopes[:, None, None], (heads, 1, tq))
    kern = functools.partial(_attn_kernel, lam_init=lam_init, tq=tq)
    return pl.pallas_call(
        kern,
        grid=(bsz, heads, nq),
        in_specs=[pl.BlockSpec((4, ATT_DK), lambda b, h, i: (0, 0)),
                  pl.BlockSpec((1, ATT_DV), lambda b, h, i: (0, 0)),
                  pl.BlockSpec((1, 1, tq), lambda b, h, i: (h, 0, 0)),
                  pl.BlockSpec((tq, ATT_DV), lambda b, h, i: (b * nq + i, h)),
                  pl.BlockSpec((seq, ATT_DV), lambda b, h, i: (b, heads + h)),
                  pl.BlockSpec((seq, ATT_DV), lambda b, h, i: (b, 2 * heads + h))],
        out_specs=pl.BlockSpec((tq, ATT_DV), lambda b, h, i: (b * nq + i, h)),
        out_shape=jax.ShapeDtypeStruct((t, d), F32),
        scratch_shapes=[pltpu.VMEM((2, tq, 1), F32), pltpu.VMEM((2, tq, 1), F32),
                        pltpu.VMEM((2, tq, ATT_DV), F32)],
        compiler_params=_cparams(("arbitrary", "arbitrary", "arbitrary")),
        name="diff_attention",
    )(lam_qk, subln_g.reshape(1, ATT_DV), slopes, z_att, z_att, z_att)


def _pool_kernel(p_ref, w_ref, sc_ref, o_ref, pad_scr, *, seq):
    g = pl.program_id(1)
    pad = max(POOL_WINDOWS)
    p = p_ref[...]
    pad_scr[0:pad, :] = jnp.zeros((pad, p.shape[1]), F32)
    pad_scr[pad:pad + seq, :] = p
    t1 = lax.broadcasted_iota(jnp.int32, (seq, 1), 0) + 1

    for gi, win in enumerate(POOL_WINDOWS):
        @pl.when(g == gi)
        def _(win=win):
            acc = p
            for s in range(1, win):
                acc = acc + pad_scr[pad - s:pad - s + seq, :]
            count = jnp.minimum(t1, win).astype(F32)
            pooled = acc / count - p
            y = _dot(pooled.astype(BF16), w_ref[0])
            o_ref[...] = y * sc_ref[0]


def _multiscale_pool(z_rest, pool_w, pool_scale, bsz, seq):
    t = z_rest.shape[0]
    groups, gw, _ = pool_w.shape
    kern = functools.partial(_pool_kernel, seq=seq)
    return pl.pallas_call(
        kern,
        grid=(bsz, groups),
        in_specs=[pl.BlockSpec((seq, gw), lambda b, g: (b, g)),
                  pl.BlockSpec((1, gw, gw), lambda b, g: (g, 0, 0)),
                  pl.BlockSpec((1, 1, gw), lambda b, g: (g, 0, 0))],
        out_specs=pl.BlockSpec((seq, gw), lambda b, g: (b, g)),
        out_shape=jax.ShapeDtypeStruct((t, groups * gw), F32),
        scratch_shapes=[pltpu.VMEM((seq + max(POOL_WINDOWS), gw), F32)],
        compiler_params=_cparams(("arbitrary", "arbitrary")),
        name="multiscale_pool",
    )(z_rest, pool_w.astype(BF16), pool_scale.reshape(groups, 1, gw))


def _log_sigmoid(z):
    return jnp.minimum(z, 0.0) - jnp.log1p(jnp.exp(-jnp.abs(z)))


def _logaddexp(a, b):
    return jnp.maximum(a, b) + jnp.log1p(jnp.exp(-jnp.abs(a - b)))


def _hgrn_kernel(lbp_ref, ng_ref, q_ref, f_ref, i_ref, g_ref, o_ref, st_scr, b_scr, k_scr, v_scr,
                 *, layer, seq):
    chunk, sub = HGRN_CHUNK, HGRN_SUB
    lbp = lbp_ref[...]
    e = jnp.exp(lbp - jnp.max(lbp, axis=0, keepdims=True))
    prob = e / jnp.sum(e, axis=0, keepdims=True)
    cs = prob[0:1]
    for l in range(1, layer + 1):
        cs = cs + prob[l:l + 1]
    lb = cs - prob[0:1]
    log_lb = jnp.log(jnp.maximum(lb, LB_FLOOR))
    log_1m = jnp.log1p(-lb)

    tril = (lax.broadcasted_iota(jnp.int32, (chunk, chunk), 0)
            >= lax.broadcasted_iota(jnp.int32, (chunk, chunk), 1)).astype(F32)
    trow = lax.broadcasted_iota(jnp.int32, (sub, 1), 0)
    st_scr[...] = jnp.zeros(st_scr.shape, F32)
    ng = ng_ref[...]

    def chunk_body(c, carry):
        r0 = pl.multiple_of(c * chunk, chunk)
        rows = pl.ds(r0, chunk)
        logf = _logaddexp(log_lb, log_1m + _log_sigmoid(f_ref[rows, :]))
        b = _dot(tril, logf, precision=lax.Precision.HIGHEST)
        kk = 1.0 - jnp.exp(logf)
        qs = _silu(q_ref[rows, :])
        v = i_ref[rows, :]
        b_scr[...] = b
        k_scr[...] = kk
        v_scr[...] = v
        st = st_scr[...]
        vb = v.astype(BF16)
        o_inter = _dot_nt((qs * jnp.exp(b)).astype(BF16), st.astype(BF16))

        outs = []
        for blk in range(chunk // sub):
            s0 = blk * sub
            b_i = b[s0:s0 + sub]
            q_i = qs[s0:s0 + sub]
            o_i = o_inter[s0:s0 + sub]
            if blk > 0:
                b_ref0 = b[s0:s0 + 1]
                k_prev = kk[:s0] * jnp.exp(b_ref0 - b[:s0])
                a = _dot_nt((q_i * jnp.exp(b_i - b_ref0)).astype(BF16), k_prev.astype(BF16))
                o_i = o_i + _dot(a.astype(BF16), vb[:s0])
            for s in range(sub):
                b_s = b_scr[s0 + s:s0 + s + 1, :]
                k_s = k_scr[s0 + s:s0 + s + 1, :]
                v_s = v_scr[s0 + s:s0 + s + 1, :]
                dec = jnp.exp(jnp.where(trow >= s, b_i - b_s, MASK_VALUE))
                colv = jnp.sum(q_i * dec * k_s, axis=1, keepdims=True)
                o_i = o_i + colv * v_s
            outs.append(o_i)
        o = jnp.concatenate(outs, axis=0)
        o_ref[rows, :] = _rms(o, ng) * _silu(g_ref[rows, :])

        b_last = b[chunk - 1:chunk]
        k_dec = kk * jnp.exp(b_last - b)
        st_scr[...] = jnp.exp(b_last) * st + _dot_tn(vb, k_dec.astype(BF16))
        return carry

    lax.fori_loop(0, seq // chunk, chunk_body, 0)


def _hgrn2(z_rest, lower_bounds, norm_g, layer, bsz, seq, d_model):
    t = z_rest.shape[0]
    heads = d_model // HGRN_D
    depth = lower_bounds.shape[0]
    kern = functools.partial(_hgrn_kernel, layer=layer, seq=seq)

    def col(k):
        return pl.BlockSpec((seq, HGRN_D), lambda b, h, k=k: (b, k * heads + h))

    return pl.pallas_call(
        kern,
        grid=(bsz, heads),
        in_specs=[pl.BlockSpec((depth, HGRN_D), lambda b, h: (0, h)),
                  pl.BlockSpec((1, HGRN_D), lambda b, h: (0, 0)),
                  col(1), col(2), col(3), col(4)],
        out_specs=pl.BlockSpec((seq, HGRN_D), lambda b, h: (b, h)),
        out_shape=jax.ShapeDtypeStruct((t, d_model), F32),
        scratch_shapes=[pltpu.VMEM((HGRN_D, HGRN_D), F32),
                        pltpu.VMEM((HGRN_CHUNK, HGRN_D), F32),
                        pltpu.VMEM((HGRN_CHUNK, HGRN_D), F32),
                        pltpu.VMEM((HGRN_CHUNK, HGRN_D), F32)],
        compiler_params=_cparams(("arbitrary", "arbitrary")),
        name="hgrn2",
    )(lower_bounds, norm_g.reshape(1, HGRN_D), z_rest, z_rest, z_rest, z_rest)


def _merge_kernel(a_ref, p_ref, r_ref, ga_ref, gp_ref, gr_ref, x_ref, g1_ref, w_ref, o_ref):
    merged = (jax.nn.sigmoid(ga_ref[...]) * a_ref[...] + jax.nn.sigmoid(gp_ref[...]) * p_ref[...]
              + jax.nn.sigmoid(gr_ref[...]) * r_ref[...])
    o_ref[...] = x_ref[...] + g1_ref[0] * _dot(merged.astype(BF16), w_ref[...])


def _merge_out(a_out, p_out, r_out, z_rest, x2, g1, w_out, seq):
    t, d = x2.shape
    tm = min(256, seq)
    row = pl.BlockSpec((tm, d), lambda i: (i, 0))

    def gate(k):
        return pl.BlockSpec((tm, d), lambda i, k=k: (i, 5 + k))

    return pl.pallas_call(
        _merge_kernel,
        grid=(t // tm,),
        in_specs=[row, row, row, gate(0), gate(1), gate(2), row,
                  pl.BlockSpec((1, 1, d), lambda i: ((i * tm) // seq, 0, 0)),
                  pl.BlockSpec((d, d), lambda i: (0, 0))],
        out_specs=row,
        out_shape=jax.ShapeDtypeStruct((t, d), F32),
        compiler_params=_cparams(("arbitrary",)),
        name="merge_out",
    )(a_out, p_out, r_out, z_rest, z_rest, z_rest, x2, g1, w_out.astype(BF16))


def _peer_query_kernel(x_ref, g_ref, sc_ref, sh_ref, wq_ref, keys_ref, h_ref, s_ref):
    h = (_rms(x_ref[...], g_ref[...]) * (1.0 + sc_ref[0]) + sh_ref[0]).astype(BF16)
    h_ref[...] = h
    dq = 2 * PEER_DHALF
    for hd in range(PEER_HEADS):
        q = _dot(h, wq_ref[:, hd * dq:(hd + 1) * dq]).astype(BF16)
        for p in range(2):
            s_ref[hd, p] = _dot_nt(keys_ref[p], q[:, p * PEER_DHALF:(p + 1) * PEER_DHALF])


def _peer_query(x2, g, sc, sh, w_query, sub_keys, seq):
    t, d = x2.shape
    tm = min(512, seq)
    nq = w_query.shape[1]
    return pl.pallas_call(
        _peer_query_kernel,
        grid=(t // tm,),
        in_specs=[pl.BlockSpec((tm, d), lambda i: (i, 0)),
                  pl.BlockSpec((1, d), lambda i: (0, 0)),
                  pl.BlockSpec((1, 1, d), lambda i: ((i * tm) // seq, 0, 0)),
                  pl.BlockSpec((1, 1, d), lambda i: ((i * tm) // seq, 0, 0)),
                  pl.BlockSpec((d, nq), lambda i: (0, 0)),
                  pl.BlockSpec((2, PEER_NKEYS, PEER_DHALF), lambda i: (0, 0, 0))],
        out_specs=[pl.BlockSpec((tm, d), lambda i: (i, 0)),
                   pl.BlockSpec((PEER_HEADS, 2, PEER_NKEYS, tm), lambda i: (0, 0, 0, i))],
        out_shape=[jax.ShapeDtypeStruct((t, d), BF16),
                   jax.ShapeDtypeStruct((PEER_HEADS, 2, PEER_NKEYS, t), F32)],
        compiler_params=_cparams(("arbitrary",)),
        name="peer_query",
    )(x2, g.reshape(1, d), sc, sh, w_query.astype(BF16), sub_keys.astype(BF16))


NEG_BIG = -3.0e38


def _top_values(s, n):
    vals = []
    cur = s
    for k in range(n):
        m = jnp.max(cur, axis=0, keepdims=True)
        vals.append(m)
        if k + 1 < n:
            cur = jnp.where(cur == m, NEG_BIG, cur)
    return vals


_PEER_N = PEER_TOPK + 1
_PEER_PAIRS = [(i, j) for i in range(_PEER_N) for j in range(_PEER_N) if (i + 1) * (j + 1) <= _PEER_N]
_PEER_CAND_ROWS = -(-len(_PEER_PAIRS) // 8) * 8


def _peer_select_kernel(s_ref, c_ref, g_ref, eb_ref, cand_scr, wts_scr):
    n = _PEER_N
    tt = cand_scr.shape[1]
    npairs = len(_PEER_PAIRS)
    cand_scr[npairs:, :] = jnp.full((_PEER_CAND_ROWS - npairs, tt), NEG_BIG, F32)
    wts_scr[npairs:, :] = jnp.zeros((_PEER_CAND_ROWS - npairs, tt), F32)
    for hd in range(PEER_HEADS):
        sa = s_ref[hd, 0]
        sb = s_ref[hd, 1]
        ta = _top_values(sa, n)
        tb = _top_values(sb, n)
        ea_top = [jnp.exp(v - ta[0]) for v in ta]
        eb_top = [jnp.exp(v - tb[0]) for v in tb]
        for k, (i, j) in enumerate(_PEER_PAIRS):
            cand_scr[k:k + 1, :] = ta[i] + tb[j]
            wts_scr[k:k + 1, :] = ea_top[i] * eb_top[j]
        cand = cand_scr[...]
        top = _top_values(cand, n)
        tau = 0.5 * (top[PEER_TOPK - 1] + top[PEER_TOPK])
        zsum = jnp.sum(jnp.where(cand > tau, wts_scr[...], 0.0), axis=0, keepdims=True)
        c_ref[hd] = tau - sa
        g_ref[hd] = jnp.exp(sa - ta[0]) / zsum
        eb_ref[hd] = jnp.exp(sb - tb[0])


def _peer_select(scores):
    heads, _, nk, t = scores.shape
    tt = min(256, t)
    spec = pl.BlockSpec((heads, nk, tt), lambda i: (0, 0, i))
    shape = jax.ShapeDtypeStruct((heads, nk, t), F32)
    return pl.pallas_call(
        _peer_select_kernel,
        grid=(t // tt,),
        in_specs=[pl.BlockSpec((heads, 2, nk, tt), lambda i: (0, 0, 0, i))],
        out_specs=[spec, spec, spec],
        out_shape=[shape, shape, shape],
        scratch_shapes=[pltpu.VMEM((_PEER_CAND_ROWS, tt), F32), pltpu.VMEM((_PEER_CAND_ROWS, tt), F32)],
        compiler_params=_cparams(("arbitrary",)),
        name="peer_select",
    )(scores)


def _peer_dense_kernel(h_ref, u_ref, vt_ref, sb_ref, eb_ref, c_ref, g_ref, x_ref, g2_ref, fg_ref,
                       o_ref, acc_scr, act_scr, p_scr, *, final_norm, rows_per_step, tt):
    e = pl.program_id(1)

    @pl.when(e == 0)
    def _():
        acc_scr[...] = jnp.zeros(acc_scr.shape, F32)

    act_scr[...] = _dot_nt(u_ref[...], h_ref[...])

    for r in range(rows_per_step):
        e0 = r * PEER_NKEYS
        for lc in range(tt // LANES):
            ls = slice(lc * LANES, (lc + 1) * LANES)
            w = jnp.zeros((PEER_NKEYS, LANES), F32)
            for hd in range(PEER_HEADS):
                w = w + jnp.where(sb_ref[hd, :, ls] > c_ref[hd, r:r + 1, ls],
                                  eb_ref[hd, :, ls] * g_ref[hd, r:r + 1, ls], 0.0)
            a = act_scr[e0:e0 + PEER_NKEYS, ls]
            gelu = 0.5 * a * (1.0 + lax.erf(a * (2.0 ** -0.5)))
            p_scr[e0:e0 + PEER_NKEYS, ls] = (w * gelu).astype(BF16)
    acc_scr[...] += _dot(vt_ref[...], p_scr[...])

    @pl.when(e == pl.num_programs(1) - 1)
    def _():
        y = x_ref[...] + g2_ref[0] * acc_scr[...].T
        if final_norm:
            y = _rms(y, fg_ref[...])
        o_ref[...] = y


def _peer_dense(h_bf, u_tab, v_tab, scores, c_arr, g_arr, eb_arr, x2, g2, final_g, final_norm, seq):
    t, d = x2.shape
    n_exp = u_tab.shape[0]
    heads, _, nk, _ = scores.shape
    tt = min(512, seq)
    rows_per_step = 8
    et = rows_per_step * nk
    kern = functools.partial(_peer_dense_kernel, final_norm=final_norm,
                             rows_per_step=rows_per_step, tt=tt)
    return pl.pallas_call(
        kern,
        grid=(t // tt, n_exp // et),
        in_specs=[pl.BlockSpec((tt, d), lambda i, e: (i, 0)),
                  pl.BlockSpec((et, d), lambda i, e: (e, 0)),
                  pl.BlockSpec((d, et), lambda i, e: (0, e)),
                  pl.BlockSpec((heads, None, nk, tt), lambda i, e: (0, 1, 0, i)),
                  pl.BlockSpec((heads, nk, tt), lambda i, e: (0, 0, i)),
                  pl.BlockSpec((heads, rows_per_step, tt), lambda i, e: (0, e, i)),
                  pl.BlockSpec((heads, rows_per_step, tt), lambda i, e: (0, e, i)),
                  pl.BlockSpec((tt, d), lambda i, e: (i, 0)),
                  pl.BlockSpec((1, 1, d), lambda i, e: ((i * tt) // seq, 0, 0)),
                  pl.BlockSpec((1, d), lambda i, e: (0, 0))],
        out_specs=pl.BlockSpec((tt, d), lambda i, e: (i, 0)),
        out_shape=jax.ShapeDtypeStruct((t, d), F32),
        scratch_shapes=[pltpu.VMEM((d, tt), F32), pltpu.VMEM((et, tt), F32),
                        pltpu.VMEM((et, tt), BF16)],
        compiler_params=_cparams(("arbitrary", "arbitrary")),
        name="peer_dense",
    )(h_bf, u_tab.astype(BF16), v_tab.astype(BF16).T, scores, eb_arr, c_arr, g_arr, x2, g2,
      final_g.reshape(1, d))


def kernel(x, c, ada_w, ada_b, norm_mix_g, w_in, diff_lambda, diff_subln_g, pool_w, pool_scale,
           hgrn_lower_bounds, hgrn_norm_g, w_out, norm_ffn_g, peer_w_query, peer_sub_keys,
           peer_u, peer_v, final_g):
    bsz, seq, d = x.shape
    depth = ada_w.shape[0]
    att_cols = 3 * d
    x2 = x.reshape(bsz * seq, d)
    ada = _ada(c, ada_w, ada_b)
    for l in range(depth):
        sh1, sc1, g1, sh2, sc2, g2 = [ada[l, :, None, k * d:(k + 1) * d] for k in range(6)]
        w_l = w_in[l].astype(BF16)
        z_att = _norm_mod_matmul(x2, norm_mix_g[l], sc1, sh1, w_l[:, :att_cols], BF16, seq)
        z_rest = _norm_mod_matmul(x2, norm_mix_g[l], sc1, sh1, w_l[:, att_cols:], F32, seq)
        a_out = _diff_attention(z_att, diff_lambda[l], diff_subln_g[l], l, bsz, seq)
        p_out = _multiscale_pool(z_rest, pool_w[l], pool_scale[l], bsz, seq)
        r_out = _hgrn2(z_rest, hgrn_lower_bounds, hgrn_norm_g[l], l, bsz, seq, d)
        x2 = _merge_out(a_out, p_out, r_out, z_rest, x2, g1, w_out[l], seq)
        h_bf, scores = _peer_query(x2, norm_ffn_g[l], sc2, sh2, peer_w_query[l], peer_sub_keys[l], seq)
        c_arr, g_arr, eb_arr = _peer_select(scores)
        x2 = _peer_dense(h_bf, peer_u[l], peer_v[l], scores, c_arr, g_arr, eb_arr, x2, g2,
                         final_g, l == depth - 1, seq)
    return x2.reshape(bsz, seq, d)
```

```python
import functools
import math

import jax
import jax.numpy as jnp
from jax import lax
from jax.experimental import pallas as pl
from jax.experimental.pallas import tpu as pltpu

F32 = jnp.float32
BF16 = jnp.bfloat16

RMS_EPS = 1e-6
MASK_VALUE = -1e30
LB_FLOOR = 1e-20
LANES = 128

ATT_DK = 64
ATT_DV = 2 * ATT_DK
POOL_WINDOWS = (2, 4, 8, 16)
HGRN_D = 128
HGRN_CHUNK = 64
HGRN_SUB = 16
PEER_HEADS = 8
PEER_NKEYS = 128
PEER_TOPK = 16
PEER_DHALF = 128

VMEM_LIMIT = 56 * 1024 * 1024


def _cparams(sem):
    return pltpu.CompilerParams(dimension_semantics=sem, vmem_limit_bytes=VMEM_LIMIT)


def _dot(a, b, **kw):
    return jnp.dot(a, b, preferred_element_type=F32, **kw)


def _dot_nt(a, b):
    return lax.dot_general(a, b, (((1,), (1,)), ((), ())), preferred_element_type=F32)


def _dot_tn(a, b):
    return lax.dot_general(a, b, (((0,), (0,)), ((), ())), preferred_element_type=F32)


def _rms(x, g):
    return x * lax.rsqrt(jnp.mean(x * x, axis=-1, keepdims=True) + RMS_EPS) * g


def _silu(x):
    return x * jax.nn.sigmoid(x)


def _ada_kernel(c_ref, w_ref, b_ref, o_ref):
    o_ref[0] = _dot(_silu(c_ref[...]), w_ref[0], precision=lax.Precision.HIGHEST) + b_ref[0]


def _ada(c, ada_w, ada_b):
    depth, d, n = ada_w.shape
    bsz = c.shape[0]
    tn = 1024
    return pl.pallas_call(
        _ada_kernel,
        grid=(depth, n // tn),
        in_specs=[pl.BlockSpec((bsz, d), lambda l, j: (0, 0)),
                  pl.BlockSpec((1, d, tn), lambda l, j: (l, 0, j)),
                  pl.BlockSpec((1, 1, tn), lambda l, j: (l, 0, j))],
        out_specs=pl.BlockSpec((1, bsz, tn), lambda l, j: (l, 0, j)),
        out_shape=jax.ShapeDtypeStruct((depth, bsz, n), F32),
        compiler_params=_cparams(("arbitrary", "arbitrary")),
        name="ada_proj",
    )(c, ada_w, ada_b.reshape(depth, 1, n))


def _norm_mod_matmul_kernel(x_ref, g_ref, sc_ref, sh_ref, w_ref, o_ref, h_scr):
    @pl.when(pl.program_id(1) == 0)
    def _():
        h = _rms(x_ref[...], g_ref[...]) * (1.0 + sc_ref[0]) + sh_ref[0]
        h_scr[...] = h.astype(BF16)

    o_ref[...] = _dot(h_scr[...], w_ref[...]).astype(o_ref.dtype)


def _norm_mod_matmul(x2, g, sc, sh, w, out_dtype, seq):
    t, d = x2.shape
    n = w.shape[1]
    tm = min(1024, seq)
    tn = 1024
    return pl.pallas_call(
        _norm_mod_matmul_kernel,
        grid=(t // tm, n // tn),
        in_specs=[pl.BlockSpec((tm, d), lambda i, j: (i, 0)),
                  pl.BlockSpec((1, d), lambda i, j: (0, 0)),
                  pl.BlockSpec((1, 1, d), lambda i, j: ((i * tm) // seq, 0, 0)),
                  pl.BlockSpec((1, 1, d), lambda i, j: ((i * tm) // seq, 0, 0)),
                  pl.BlockSpec((d, tn), lambda i, j: (0, j))],
        out_specs=pl.BlockSpec((tm, tn), lambda i, j: (i, j)),
        out_shape=jax.ShapeDtypeStruct((t, n), out_dtype),
        scratch_shapes=[pltpu.VMEM((tm, d), BF16)],
        compiler_params=_cparams(("arbitrary", "arbitrary")),
        name="norm_mod_matmul",
    )(x2, g.reshape(1, d), sc, sh, w)


LOG2E = 1.4426950408889634


def _attn_kernel(lam_ref, g_ref, slope_ref, q_ref, k_ref, v_ref, o_ref, s_scr, mx_scr, ls_scr, acc_scr,
                 *, lam_init, tq):
    qi = pl.program_id(2)
    slope = slope_ref[0] * LOG2E
    lane = lax.broadcasted_iota(jnp.int32, (1, 2 * ATT_DK), 1)
    q = (q_ref[...].astype(F32) * (ATT_DK ** -0.5 * LOG2E)).astype(BF16)
    zero = jnp.zeros_like(q)
    qq = jnp.concatenate([jnp.where(lane < ATT_DK, q, zero), jnp.where(lane >= ATT_DK, q, zero)], axis=0)
    col = lax.broadcasted_iota(jnp.int32, (1, tq), 1)

    def logits(j):
        k = k_ref[pl.ds(pl.multiple_of(j * tq, tq), tq), :]
        bias = slope * (col + (j - qi) * tq).astype(F32)
        return _dot_nt(qq, k) + bias

    def fold_max(s):
        m = s[:, :LANES]
        for c in range(1, tq // LANES):
            m = jnp.maximum(m, s[:, c * LANES:(c + 1) * LANES])
        mx_scr[...] = jnp.maximum(mx_scr[...], m)

    mx_scr[...] = jnp.full(mx_scr.shape, MASK_VALUE, F32)

    def pass_a(j, carry):
        s = logits(j)
        s_scr[j] = s
        fold_max(s)
        return carry

    lax.fori_loop(0, qi, pass_a, 0)
    row = lax.broadcasted_iota(jnp.int32, (2 * tq, tq), 0) & (tq - 1)
    keep = row >= lax.broadcasted_iota(jnp.int32, (2 * tq, tq), 1)
    s = jnp.where(keep, logits(qi), MASK_VALUE)
    s_scr[qi] = s
    fold_max(s)

    m_row = jnp.max(mx_scr[...], axis=1, keepdims=True)
    mx_scr[...] = jnp.broadcast_to(m_row, mx_scr.shape)
    ls_scr[...] = jnp.zeros(ls_scr.shape, F32)
    acc_scr[...] = jnp.zeros(acc_scr.shape, F32)

    def pass_b(j, carry):
        s = s_scr[j]
        m = mx_scr[...]
        ps = [jnp.exp2(s[:, c * LANES:(c + 1) * LANES] - m) for c in range(tq // LANES)]
        tot = ps[0]
        for p in ps[1:]:
            tot = tot + p
        ls_scr[...] += tot
        p = jnp.concatenate(ps, axis=1).astype(BF16)
        v = v_ref[pl.ds(pl.multiple_of(j * tq, tq), tq), :]
        acc_scr[...] += _dot(p, v)
        return carry

    lax.fori_loop(0, qi + 1, pass_b, 0)

    lq = lam_ref[...]
    lam = (jnp.exp(jnp.sum(lq[0:1] * lq[1:2], axis=1, keepdims=True))
           - jnp.exp(jnp.sum(lq[2:3] * lq[3:4], axis=1, keepdims=True)) + lam_init)
    o2 = acc_scr[...] / jnp.sum(ls_scr[...], axis=1, keepdims=True)
    o = o2[:tq] - lam * o2[tq:]
    o_ref[...] = _rms(o, g_ref[...]) * (1.0 - lam_init)


def _diff_attention(z_att, lam_qk, subln_g, layer, bsz, seq):
    t = z_att.shape[0]
    d = z_att.shape[1] // 3
    heads = d // ATT_DV
    tq = min(256, seq)
    nq = seq // tq
    lam_init = 0.8 - 0.6 * math.exp(-0.3 * layer)
    slopes = jnp.asarray([2.0 ** (-8.0 * (h + 1) / heads) for h in range(heads)], F32)
    slopes = jnp.broadcast_to(slopes[:, None, None], (heads, 1, tq))
    kern = functools.partial(_attn_kernel, lam_init=lam_init, tq=tq)
    return pl.pallas_call(
        kern,
        grid=(bsz, heads, nq),
        in_specs=[pl.BlockSpec((4, ATT_DK), lambda b, h, i: (0, 0)),
                  pl.BlockSpec((1, ATT_DV), lambda b, h, i: (0, 0)),
                  pl.BlockSpec((1, 1, tq), lambda b, h, i: (h, 0, 0)),
                  pl.BlockSpec((tq, ATT_DV), lambda b, h, i: (b * nq + i, h)),
                  pl.BlockSpec((seq, ATT_DV), lambda b, h, i: (b, heads + h)),
                  pl.BlockSpec((seq, ATT_DV), lambda b, h, i: (b, 2 * heads + h))],
        out_specs=pl.BlockSpec((tq, ATT_DV), lambda b, h, i: (b * nq + i, h)),
        out_shape=jax.ShapeDtypeStruct((t, d), F32),
        scratch_shapes=[pltpu.VMEM((nq, 2 * tq, tq), F32), pltpu.VMEM((2 * tq, LANES), F32),
                        pltpu.VMEM((2 * tq, LANES), F32), pltpu.VMEM((2 * tq, ATT_DV), F32)],
        compiler_params=_cparams(("arbitrary", "arbitrary", "arbitrary")),
        name="diff_attention",
    )(lam_qk, subln_g.reshape(1, ATT_DV), slopes, z_att, z_att, z_att)


def _pool_kernel(p_ref, w_ref, sc_ref, o_ref, pad_scr, *, seq):
    g = pl.program_id(1)
    pad = max(POOL_WINDOWS)
    p = p_ref[...]
    pad_scr[0:pad, :] = jnp.zeros((pad, p.shape[1]), F32)
    pad_scr[pad:pad + seq, :] = p
    t1 = lax.broadcasted_iota(jnp.int32, (seq, 1), 0) + 1

    for gi, win in enumerate(POOL_WINDOWS):
        @pl.when(g == gi)
        def _(win=win):
            acc = p
            for s in range(1, win):
                acc = acc + pad_scr[pad - s:pad - s + seq, :]
            count = jnp.minimum(t1, win).astype(F32)
            pooled = acc / count - p
            y = _dot(pooled.astype(BF16), w_ref[0])
            o_ref[...] = y * sc_ref[0]


def _multiscale_pool(z_rest, pool_w, pool_scale, bsz, seq):
    t = z_rest.shape[0]
    groups, gw, _ = pool_w.shape
    kern = functools.partial(_pool_kernel, seq=seq)
    return pl.pallas_call(
        kern,
        grid=(bsz, groups),
        in_specs=[pl.BlockSpec((seq, gw), lambda b, g: (b, g)),
                  pl.BlockSpec((1, gw, gw), lambda b, g: (g, 0, 0)),
                  pl.BlockSpec((1, 1, gw), lambda b, g: (g, 0, 0))],
        out_specs=pl.BlockSpec((seq, gw), lambda b, g: (b, g)),
        out_shape=jax.ShapeDtypeStruct((t, groups * gw), F32),
        scratch_shapes=[pltpu.VMEM((seq + max(POOL_WINDOWS), gw), F32)],
        compiler_params=_cparams(("arbitrary", "arbitrary")),
        name="multiscale_pool",
    )(z_rest, pool_w.astype(BF16), pool_scale.reshape(groups, 1, gw))


def _log_sigmoid(z):
    return jnp.minimum(z, 0.0) - jnp.log1p(jnp.exp(-jnp.abs(z)))


def _logaddexp(a, b):
    return jnp.maximum(a, b) + jnp.log1p(jnp.exp(-jnp.abs(a - b)))


def _hgrn_kernel(lbp_ref, ng_ref, q_ref, f_ref, i_ref, g_ref, o_ref, st_scr, b_scr, k_scr, v_scr,
                 *, layer, seq, hb):
    chunk, sub = HGRN_CHUNK, HGRN_SUB
    lbp = lbp_ref[...]
    e = jnp.exp(lbp - jnp.max(lbp, axis=0, keepdims=True))
    prob = e / jnp.sum(e, axis=0, keepdims=True)
    cs = prob[0:1]
    for l in range(1, layer + 1):
        cs = cs + prob[l:l + 1]
    lb = cs - prob[0:1]
    log_lb = jnp.log(jnp.maximum(lb, LB_FLOOR))
    log_1m = jnp.log1p(-lb)

    tril = (lax.broadcasted_iota(jnp.int32, (chunk, chunk), 0)
            >= lax.broadcasted_iota(jnp.int32, (chunk, chunk), 1)).astype(F32)
    trow = lax.broadcasted_iota(jnp.int32, (sub, 1), 0)
    st_scr[...] = jnp.zeros(st_scr.shape, F32)
    ng = ng_ref[...]

    def head_chunk(hh, rows):
        hs = slice(hh * HGRN_D, (hh + 1) * HGRN_D)
        logf = _logaddexp(log_lb[:, hs], log_1m[:, hs] + _log_sigmoid(f_ref[rows, hs]))
        b = _dot(tril, logf, precision=lax.Precision.HIGHEST)
        kk = 1.0 - jnp.exp(logf)
        qs = _silu(q_ref[rows, hs])
        v = i_ref[rows, hs]
        b_scr[hh] = b
        k_scr[hh] = kk
        v_scr[hh] = v
        st = st_scr[hh]
        vb = v.astype(BF16)
        o_inter = _dot_nt((qs * jnp.exp(b)).astype(BF16), st.astype(BF16))

        outs = []
        for blk in range(chunk // sub):
            s0 = blk * sub
            b_i = b[s0:s0 + sub]
            q_i = qs[s0:s0 + sub]
            o_i = o_inter[s0:s0 + sub]
            if blk > 0:
                b_ref0 = b[s0:s0 + 1]
                k_prev = kk[:s0] * jnp.exp(b_ref0 - b[:s0])
                a = _dot_nt((q_i * jnp.exp(b_i - b_ref0)).astype(BF16), k_prev.astype(BF16))
                o_i = o_i + _dot(a.astype(BF16), vb[:s0])
            for s in range(sub):
                b_s = b_scr[hh, s0 + s:s0 + s + 1, :]
                k_s = k_scr[hh, s0 + s:s0 + s + 1, :]
                v_s = v_scr[hh, s0 + s:s0 + s + 1, :]
                dec = jnp.exp(jnp.where(trow >= s, b_i - b_s, MASK_VALUE))
                colv = jnp.sum(q_i * dec * k_s, axis=1, keepdims=True)
                o_i = o_i + colv * v_s
            outs.append(o_i)
        o = jnp.concatenate(outs, axis=0)
        o_ref[rows, hs] = _rms(o, ng) * _silu(g_ref[rows, hs])

        b_last = b[chunk - 1:chunk]
        k_dec = kk * jnp.exp(b_last - b)
        st_scr[hh] = jnp.exp(b_last) * st + _dot_tn(vb, k_dec.astype(BF16))

    def chunk_body(c, carry):
        rows = pl.ds(pl.multiple_of(c * chunk, chunk), chunk)
        for hh in range(hb):
            head_chunk(hh, rows)
        return carry

    lax.fori_loop(0, seq // chunk, chunk_body, 0)


def _hgrn2(z_rest, lower_bounds, norm_g, layer, bsz, seq, d_model):
    t = z_rest.shape[0]
    heads = d_model // HGRN_D
    depth = lower_bounds.shape[0]
    hb = 4
    hw = hb * HGRN_D
    ng = heads // hb
    kern = functools.partial(_hgrn_kernel, layer=layer, seq=seq, hb=hb)

    def col(k):
        return pl.BlockSpec((seq, hw), lambda b, h, k=k: (b, k * ng + h))

    return pl.pallas_call(
        kern,
        grid=(bsz, ng),
        in_specs=[pl.BlockSpec((depth, hw), lambda b, h: (0, h)),
                  pl.BlockSpec((1, HGRN_D), lambda b, h: (0, 0)),
                  col(1), col(2), col(3), col(4)],
        out_specs=pl.BlockSpec((seq, hw), lambda b, h: (b, h)),
        out_shape=jax.ShapeDtypeStruct((t, d_model), F32),
        scratch_shapes=[pltpu.VMEM((hb, HGRN_D, HGRN_D), F32),
                        pltpu.VMEM((hb, HGRN_CHUNK, HGRN_D), F32),
                        pltpu.VMEM((hb, HGRN_CHUNK, HGRN_D), F32),
                        pltpu.VMEM((hb, HGRN_CHUNK, HGRN_D), F32)],
        compiler_params=_cparams(("arbitrary", "arbitrary")),
        name="hgrn2",
    )(lower_bounds, norm_g.reshape(1, HGRN_D), z_rest, z_rest, z_rest, z_rest)


def _merge_kernel(a_ref, p_ref, r_ref, ga_ref, gp_ref, gr_ref, x_ref, g1_ref, w_ref, o_ref):
    merged = (jax.nn.sigmoid(ga_ref[...]) * a_ref[...] + jax.nn.sigmoid(gp_ref[...]) * p_ref[...]
              + jax.nn.sigmoid(gr_ref[...]) * r_ref[...])
    o_ref[...] = x_ref[...] + g1_ref[0] * _dot(merged.astype(BF16), w_ref[...])


def _merge_out(a_out, p_out, r_out, z_rest, x2, g1, w_out, seq):
    t, d = x2.shape
    tm = min(256, seq)
    row = pl.BlockSpec((tm, d), lambda i: (i, 0))

    def gate(k):
        return pl.BlockSpec((tm, d), lambda i, k=k: (i, 5 + k))

    return pl.pallas_call(
        _merge_kernel,
        grid=(t // tm,),
        in_specs=[row, row, row, gate(0), gate(1), gate(2), row,
                  pl.BlockSpec((1, 1, d), lambda i: ((i * tm) // seq, 0, 0)),
                  pl.BlockSpec((d, d), lambda i: (0, 0))],
        out_specs=row,
        out_shape=jax.ShapeDtypeStruct((t, d), F32),
        compiler_params=_cparams(("arbitrary",)),
        name="merge_out",
    )(a_out, p_out, r_out, z_rest, z_rest, z_rest, x2, g1, w_out.astype(BF16))


def _peer_query_kernel(x_ref, g_ref, sc_ref, sh_ref, wq_ref, keys_ref, h_ref, s_ref):
    h = (_rms(x_ref[...], g_ref[...]) * (1.0 + sc_ref[0]) + sh_ref[0]).astype(BF16)
    h_ref[...] = h
    dq = 2 * PEER_DHALF
    for hd in range(PEER_HEADS):
        q = _dot(h, wq_ref[:, hd * dq:(hd + 1) * dq]).astype(BF16)
        for p in range(2):
            s_ref[hd, p] = _dot_nt(keys_ref[p], q[:, p * PEER_DHALF:(p + 1) * PEER_DHALF])


def _peer_query(x2, g, sc, sh, w_query, sub_keys, seq):
    t, d = x2.shape
    tm = min(512, seq)
    nq = w_query.shape[1]
    return pl.pallas_call(
        _peer_query_kernel,
        grid=(t // tm,),
        in_specs=[pl.BlockSpec((tm, d), lambda i: (i, 0)),
                  pl.BlockSpec((1, d), lambda i: (0, 0)),
                  pl.BlockSpec((1, 1, d), lambda i: ((i * tm) // seq, 0, 0)),
                  pl.BlockSpec((1, 1, d), lambda i: ((i * tm) // seq, 0, 0)),
                  pl.BlockSpec((d, nq), lambda i: (0, 0)),
                  pl.BlockSpec((2, PEER_NKEYS, PEER_DHALF), lambda i: (0, 0, 0))],
        out_specs=[pl.BlockSpec((tm, d), lambda i: (i, 0)),
                   pl.BlockSpec((PEER_HEADS, 2, PEER_NKEYS, tm), lambda i: (0, 0, 0, i))],
        out_shape=[jax.ShapeDtypeStruct((t, d), BF16),
                   jax.ShapeDtypeStruct((PEER_HEADS, 2, PEER_NKEYS, t), F32)],
        compiler_params=_cparams(("arbitrary",)),
        name="peer_query",
    )(x2, g.reshape(1, d), sc, sh, w_query.astype(BF16), sub_keys.astype(BF16))


NEG_BIG = -3.0e38


def _top_values(s, n):
    vals = []
    cur = s
    for k in range(n):
        m = jnp.max(cur, axis=0, keepdims=True)
        vals.append(m)
        if k + 1 < n:
            cur = jnp.where(cur == m, NEG_BIG, cur)
    return vals


_PEER_N = PEER_TOPK + 1
_PEER_PAIRS = [(i, j) for i in range(_PEER_N) for j in range(_PEER_N) if (i + 1) * (j + 1) <= _PEER_N]
_PEER_CAND_ROWS = -(-len(_PEER_PAIRS) // 8) * 8


def _peer_select_kernel(s_ref, c_ref, g_ref, eb_ref, cand_scr, wts_scr):
    n = _PEER_N
    tt = cand_scr.shape[1]
    npairs = len(_PEER_PAIRS)
    cand_scr[npairs:, :] = jnp.full((_PEER_CAND_ROWS - npairs, tt), NEG_BIG, F32)
    wts_scr[npairs:, :] = jnp.zeros((_PEER_CAND_ROWS - npairs, tt), F32)
    for hd in range(PEER_HEADS):
        sa = s_ref[hd, 0]
        sb = s_ref[hd, 1]
        ta = _top_values(sa, n)
        tb = _top_values(sb, n)
        ea_top = [jnp.exp(v - ta[0]) for v in ta]
        eb_top = [jnp.exp(v - tb[0]) for v in tb]
        for k, (i, j) in enumerate(_PEER_PAIRS):
            cand_scr[k:k + 1, :] = ta[i] + tb[j]
            wts_scr[k:k + 1, :] = ea_top[i] * eb_top[j]
        cand = cand_scr[...]
        top = _top_values(cand, n)
        tau = 0.5 * (top[PEER_TOPK - 1] + top[PEER_TOPK])
        zsum = jnp.sum(jnp.where(cand > tau, wts_scr[...], 0.0), axis=0, keepdims=True)
        c_ref[hd] = tau - sa
        g_ref[hd] = jnp.exp(sa - ta[0]) / zsum
        eb_ref[hd] = jnp.exp(sb - tb[0])


def _peer_select(scores):
    heads, _, nk, t = scores.shape
    tt = min(256, t)
    spec = pl.BlockSpec((heads, nk, tt), lambda i: (0, 0, i))
    shape = jax.ShapeDtypeStruct((heads, nk, t), F32)
    return pl.pallas_call(
        _peer_select_kernel,
        grid=(t // tt,),
        in_specs=[pl.BlockSpec((heads, 2, nk, tt), lambda i: (0, 0, 0, i))],
        out_specs=[spec, spec, spec],
        out_shape=[shape, shape, shape],
        scratch_shapes=[pltpu.VMEM((_PEER_CAND_ROWS, tt), F32), pltpu.VMEM((_PEER_CAND_ROWS, tt), F32)],
        compiler_params=_cparams(("arbitrary",)),
        name="peer_select",
    )(scores)


def _peer_dense_kernel(h_ref, u_ref, vt_ref, sb_ref, eb_ref, c_ref, g_ref, x_ref, g2_ref, fg_ref,
                       o_ref, acc_scr, act_scr, p_scr, *, final_norm, rows_per_step, tt):
    e = pl.program_id(1)

    @pl.when(e == 0)
    def _():
        acc_scr[...] = jnp.zeros(acc_scr.shape, F32)

    act_scr[...] = _dot_nt(u_ref[...], h_ref[...])

    jg = 32

    def tile_body(ls, jt, carry):
        j0 = pl.multiple_of(jt * jg, jg)
        w = [jnp.zeros((jg, LANES), F32) for _ in range(rows_per_step)]
        for hd in range(PEER_HEADS):
            sbv = sb_ref[hd, pl.ds(j0, jg), ls]
            ebv = eb_ref[hd, pl.ds(j0, jg), ls]
            for r in range(rows_per_step):
                w[r] = w[r] + jnp.where(sbv > c_ref[hd, r:r + 1, ls], ebv * g_ref[hd, r:r + 1, ls], 0.0)
        for r in range(rows_per_step):
            rows = pl.ds(pl.multiple_of(r * PEER_NKEYS + j0, jg), jg)
            a = act_scr[rows, ls]
            gelu = 0.5 * a * (1.0 + lax.erf(a * (2.0 ** -0.5)))
            p_scr[rows, ls] = (w[r] * gelu).astype(BF16)
        return carry

    for lc in range(tt // LANES):
        ls = slice(lc * LANES, (lc + 1) * LANES)
        lax.fori_loop(0, PEER_NKEYS // jg, functools.partial(tile_body, ls), 0)
    acc_scr[...] += _dot(vt_ref[...], p_scr[...])

    @pl.when(e == pl.num_programs(1) - 1)
    def _():
        y = x_ref[...] + g2_ref[0] * acc_scr[...].T
        if final_norm:
            y = _rms(y, fg_ref[...])
        o_ref[...] = y


def _peer_dense(h_bf, u_tab, v_tab, scores, c_arr, g_arr, eb_arr, x2, g2, final_g, final_norm, seq):
    t, d = x2.shape
    n_exp = u_tab.shape[0]
    heads, _, nk, _ = scores.shape
    tt = min(512, seq)
    rows_per_step = 8
    et = rows_per_step * nk
    kern = functools.partial(_peer_dense_kernel, final_norm=final_norm,
                             rows_per_step=rows_per_step, tt=tt)
    return pl.pallas_call(
        kern,
        grid=(t // tt, n_exp // et),
        in_specs=[pl.BlockSpec((tt, d), lambda i, e: (i, 0)),
                  pl.BlockSpec((et, d), lambda i, e: (e, 0)),
                  pl.BlockSpec((d, et), lambda i, e: (0, e)),
                  pl.BlockSpec((heads, None, nk, tt), lambda i, e: (0, 1, 0, i)),
                  pl.BlockSpec((heads, nk, tt), lambda i, e: (0, 0, i)),
                  pl.BlockSpec((heads, rows_per_step, tt), lambda i, e: (0, e, i)),
                  pl.BlockSpec((heads, rows_per_step, tt), lambda i, e: (0, e, i)),
                  pl.BlockSpec((tt, d), lambda i, e: (i, 0)),
                  pl.BlockSpec((1, 1, d), lambda i, e: ((i * tt) // seq, 0, 0)),
                  pl.BlockSpec((1, d), lambda i, e: (0, 0))],
        out_specs=pl.BlockSpec((tt, d), lambda i, e: (i, 0)),
        out_shape=jax.ShapeDtypeStruct((t, d), F32),
        scratch_shapes=[pltpu.VMEM((d, tt), F32), pltpu.VMEM((et, tt), F32),
                        pltpu.VMEM((et, tt), BF16)],
        compiler_params=_cparams(("arbitrary", "arbitrary")),
        name="peer_dense",
    )(h_bf, u_tab.astype(BF16), v_tab.astype(BF16).T, scores, eb_arr, c_arr, g_arr, x2, g2,
      final_g.reshape(1, d))


def kernel(x, c, ada_w, ada_b, norm_mix_g, w_in, diff_lambda, diff_subln_g, pool_w, pool_scale,
           hgrn_lower_bounds, hgrn_norm_g, w_out, norm_ffn_g, peer_w_query, peer_sub_keys,
           peer_u, peer_v, final_g):
    bsz, seq, d = x.shape
    depth = ada_w.shape[0]
    att_cols = 3 * d
    x2 = x.reshape(bsz * seq, d)
    ada = _ada(c, ada_w, ada_b)
    for l in range(depth):
        sh1, sc1, g1, sh2, sc2, g2 = [ada[l, :, None, k * d:(k + 1) * d] for k in range(6)]
        w_l = w_in[l].astype(BF16)
        z_att = _norm_mod_matmul(x2, norm_mix_g[l], sc1, sh1, w_l[:, :att_cols], BF16, seq)
        z_rest = _norm_mod_matmul(x2, norm_mix_g[l], sc1, sh1, w_l[:, att_cols:], F32, seq)
        a_out = _diff_attention(z_att, diff_lambda[l], diff_subln_g[l], l, bsz, seq)
        p_out = _multiscale_pool(z_rest, pool_w[l], pool_scale[l], bsz, seq)
        r_out = _hgrn2(z_rest, hgrn_lower_bounds, hgrn_norm_g[l], l, bsz, seq, d)
        x2 = _merge_out(a_out, p_out, r_out, z_rest, x2, g1, w_out[l], seq)
        h_bf, scores = _peer_query(x2, norm_ffn_g[l], sc2, sh2, peer_w_query[l], peer_sub_keys[l], seq)
        c_arr, g_arr, eb_arr = _peer_select(scores)
        x2 = _peer_dense(h_bf, peer_u[l], peer_v[l], scores, c_arr, g_arr, eb_arr, x2, g2,
                         final_g, l == depth - 1, seq)
    return x2.reshape(bsz, seq, d)
```

```python
import functools
import math

import jax
import jax.numpy as jnp
from jax import lax
from jax.experimental import pallas as pl
from jax.experimental.pallas import tpu as pltpu

F32 = jnp.float32
BF16 = jnp.bfloat16

RMS_EPS = 1e-6
MASK_VALUE = -1e30
LB_FLOOR = 1e-20
LANES = 128

ATT_DK = 64
ATT_DV = 2 * ATT_DK
POOL_WINDOWS = (2, 4, 8, 16)
HGRN_D = 128
HGRN_CHUNK = 64
HGRN_SUB = 16
PEER_HEADS = 8
PEER_NKEYS = 128
PEER_TOPK = 16
PEER_DHALF = 128

VMEM_LIMIT = 56 * 1024 * 1024


def _cparams(sem):
    return pltpu.CompilerParams(dimension_semantics=sem, vmem_limit_bytes=VMEM_LIMIT)


def _dot(a, b, **kw):
    return jnp.dot(a, b, preferred_element_type=F32, **kw)


def _dot_nt(a, b):
    return lax.dot_general(a, b, (((1,), (1,)), ((), ())), preferred_element_type=F32)


def _dot_tn(a, b):
    return lax.dot_general(a, b, (((0,), (0,)), ((), ())), preferred_element_type=F32)


def _rms(x, g):
    return x * lax.rsqrt(jnp.mean(x * x, axis=-1, keepdims=True) + RMS_EPS) * g


def _silu(x):
    return x * jax.nn.sigmoid(x)


def _ada_kernel(c_ref, w_ref, b_ref, o_ref):
    o_ref[0] = _dot(_silu(c_ref[...]), w_ref[0], precision=lax.Precision.HIGHEST) + b_ref[0]


def _ada(c, ada_w, ada_b):
    depth, d, n = ada_w.shape
    bsz = c.shape[0]
    tn = 1024
    return pl.pallas_call(
        _ada_kernel,
        grid=(depth, n // tn),
        in_specs=[pl.BlockSpec((bsz, d), lambda l, j: (0, 0)),
                  pl.BlockSpec((1, d, tn), lambda l, j: (l, 0, j)),
                  pl.BlockSpec((1, 1, tn), lambda l, j: (l, 0, j))],
        out_specs=pl.BlockSpec((1, bsz, tn), lambda l, j: (l, 0, j)),
        out_shape=jax.ShapeDtypeStruct((depth, bsz, n), F32),
        compiler_params=_cparams(("arbitrary", "arbitrary")),
        name="ada_proj",
    )(c, ada_w, ada_b.reshape(depth, 1, n))


def _norm_mod_matmul_kernel(x_ref, g_ref, sc_ref, sh_ref, w_ref, o_ref, h_scr):
    @pl.when(pl.program_id(1) == 0)
    def _():
        h = _rms(x_ref[...], g_ref[...]) * (1.0 + sc_ref[0]) + sh_ref[0]
        h_scr[...] = h.astype(BF16)

    res = _dot(h_scr[...], w_ref[...]).astype(o_ref.dtype)
    if len(o_ref.shape) == 2:
        o_ref[...] = res
    else:
        for hh in range(o_ref.shape[0]):
            o_ref[hh] = res[:, hh * LANES:(hh + 1) * LANES]


def _norm_mod_matmul(x2, g, sc, sh, w, out_dtype, seq, head_major=False):
    t, d = x2.shape
    n = w.shape[1]
    tm = min(1024, seq)
    tn = 1024
    if head_major:
        out_spec = pl.BlockSpec((tn // LANES, tm, LANES), lambda i, j: (j, i, 0))
        out_shape = jax.ShapeDtypeStruct((n // LANES, t, LANES), out_dtype)
    else:
        out_spec = pl.BlockSpec((tm, tn), lambda i, j: (i, j))
        out_shape = jax.ShapeDtypeStruct((t, n), out_dtype)
    return pl.pallas_call(
        _norm_mod_matmul_kernel,
        grid=(t // tm, n // tn),
        in_specs=[pl.BlockSpec((tm, d), lambda i, j: (i, 0)),
                  pl.BlockSpec((1, d), lambda i, j: (0, 0)),
                  pl.BlockSpec((1, 1, d), lambda i, j: ((i * tm) // seq, 0, 0)),
                  pl.BlockSpec((1, 1, d), lambda i, j: ((i * tm) // seq, 0, 0)),
                  pl.BlockSpec((d, tn), lambda i, j: (0, j))],
        out_specs=out_spec,
        out_shape=out_shape,
        scratch_shapes=[pltpu.VMEM((tm, d), BF16)],
        compiler_params=_cparams(("arbitrary", "arbitrary")),
        name="norm_mod_matmul",
    )(x2, g.reshape(1, d), sc, sh, w)


LOG2E = 1.4426950408889634


def _attn_kernel(lam_ref, g_ref, slope_ref, q_ref, k_ref, v_ref, o_ref, s_scr, mx_scr, ls_scr, acc_scr,
                 *, lam_init, tq):
    qi = pl.program_id(2)
    slope = slope_ref[0] * LOG2E
    lane = lax.broadcasted_iota(jnp.int32, (1, 2 * ATT_DK), 1)
    q = (q_ref[...].astype(F32) * (ATT_DK ** -0.5 * LOG2E)).astype(BF16)
    zero = jnp.zeros_like(q)
    qq = jnp.concatenate([jnp.where(lane < ATT_DK, q, zero), jnp.where(lane >= ATT_DK, q, zero)], axis=0)
    col = lax.broadcasted_iota(jnp.int32, (1, tq), 1)

    def logits(j):
        k = k_ref[pl.ds(pl.multiple_of(j * tq, tq), tq), :]
        bias = slope * (col + (j - qi) * tq).astype(F32)
        return _dot_nt(qq, k) + bias

    def fold_max(s):
        m = s[:, :LANES]
        for c in range(1, tq // LANES):
            m = jnp.maximum(m, s[:, c * LANES:(c + 1) * LANES])
        mx_scr[...] = jnp.maximum(mx_scr[...], m)

    mx_scr[...] = jnp.full(mx_scr.shape, MASK_VALUE, F32)

    def pass_a(j, carry):
        s = logits(j)
        s_scr[j] = s
        fold_max(s)
        return carry

    lax.fori_loop(0, qi, pass_a, 0)
    row = lax.broadcasted_iota(jnp.int32, (2 * tq, tq), 0) & (tq - 1)
    keep = row >= lax.broadcasted_iota(jnp.int32, (2 * tq, tq), 1)
    s = jnp.where(keep, logits(qi), MASK_VALUE)
    s_scr[qi] = s
    fold_max(s)

    m_row = jnp.max(mx_scr[...], axis=1, keepdims=True)
    mx_scr[...] = jnp.broadcast_to(m_row, mx_scr.shape)
    ls_scr[...] = jnp.zeros(ls_scr.shape, F32)
    acc_scr[...] = jnp.zeros(acc_scr.shape, F32)

    def pass_b(j, carry):
        s = s_scr[j]
        m = mx_scr[...]
        ps = [jnp.exp2(s[:, c * LANES:(c + 1) * LANES] - m) for c in range(tq // LANES)]
        tot = ps[0]
        for p in ps[1:]:
            tot = tot + p
        ls_scr[...] += tot
        p = jnp.concatenate(ps, axis=1).astype(BF16)
        v = v_ref[pl.ds(pl.multiple_of(j * tq, tq), tq), :]
        acc_scr[...] += _dot(p, v)
        return carry

    lax.fori_loop(0, qi + 1, pass_b, 0)

    lq = lam_ref[...]
    lam = (jnp.exp(jnp.sum(lq[0:1] * lq[1:2], axis=1, keepdims=True))
           - jnp.exp(jnp.sum(lq[2:3] * lq[3:4], axis=1, keepdims=True)) + lam_init)
    o2 = acc_scr[...] / jnp.sum(ls_scr[...], axis=1, keepdims=True)
    o = o2[:tq] - lam * o2[tq:]
    o_ref[...] = _rms(o, g_ref[...]) * (1.0 - lam_init)


def _diff_attention(z_att, lam_qk, subln_g, layer, bsz, seq):
    t = z_att.shape[1]
    heads = z_att.shape[0] // 3
    tq = min(256, seq)
    nq = seq // tq
    lam_init = 0.8 - 0.6 * math.exp(-0.3 * layer)
    slopes = jnp.asarray([2.0 ** (-8.0 * (h + 1) / heads) for h in range(heads)], F32)
    slopes = jnp.broadcast_to(slopes[:, None, None], (heads, 1, tq))
    kern = functools.partial(_attn_kernel, lam_init=lam_init, tq=tq)
    return pl.pallas_call(
        kern,
        grid=(bsz, heads, nq),
        in_specs=[pl.BlockSpec((4, ATT_DK), lambda b, h, i: (0, 0)),
                  pl.BlockSpec((1, ATT_DV), lambda b, h, i: (0, 0)),
                  pl.BlockSpec((1, 1, tq), lambda b, h, i: (h, 0, 0)),
                  pl.BlockSpec((None, tq, ATT_DV), lambda b, h, i: (h, b * nq + i, 0)),
                  pl.BlockSpec((None, seq, ATT_DV), lambda b, h, i: (heads + h, b, 0)),
                  pl.BlockSpec((None, seq, ATT_DV), lambda b, h, i: (2 * heads + h, b, 0))],
        out_specs=pl.BlockSpec((None, tq, ATT_DV), lambda b, h, i: (h, b * nq + i, 0)),
        out_shape=jax.ShapeDtypeStruct((heads, t, ATT_DV), F32),
        scratch_shapes=[pltpu.VMEM((nq, 2 * tq, tq), F32), pltpu.VMEM((2 * tq, LANES), F32),
                        pltpu.VMEM((2 * tq, LANES), F32), pltpu.VMEM((2 * tq, ATT_DV), F32)],
        compiler_params=_cparams(("arbitrary", "arbitrary", "arbitrary")),
        name="diff_attention",
    )(lam_qk, subln_g.reshape(1, ATT_DV), slopes, z_att, z_att, z_att)


def _pool_kernel(p_ref, w_ref, sc_ref, o_ref, pad_scr, *, seq):
    g = pl.program_id(1)
    pad = max(POOL_WINDOWS)
    p = p_ref[...]
    pad_scr[0:pad, :] = jnp.zeros((pad, p.shape[1]), F32)
    pad_scr[pad:pad + seq, :] = p
    t1 = lax.broadcasted_iota(jnp.int32, (seq, 1), 0) + 1

    for gi, win in enumerate(POOL_WINDOWS):
        @pl.when(g == gi)
        def _(win=win):
            acc = p
            for s in range(1, win):
                acc = acc + pad_scr[pad - s:pad - s + seq, :]
            count = jnp.minimum(t1, win).astype(F32)
            pooled = acc / count - p
            y = _dot(pooled.astype(BF16), w_ref[0])
            o_ref[...] = y * sc_ref[0]


def _multiscale_pool(z_rest, pool_w, pool_scale, bsz, seq):
    t = z_rest.shape[0]
    groups, gw, _ = pool_w.shape
    kern = functools.partial(_pool_kernel, seq=seq)
    return pl.pallas_call(
        kern,
        grid=(bsz, groups),
        in_specs=[pl.BlockSpec((seq, gw), lambda b, g: (b, g)),
                  pl.BlockSpec((1, gw, gw), lambda b, g: (g, 0, 0)),
                  pl.BlockSpec((1, 1, gw), lambda b, g: (g, 0, 0))],
        out_specs=pl.BlockSpec((seq, gw), lambda b, g: (b, g)),
        out_shape=jax.ShapeDtypeStruct((t, groups * gw), F32),
        scratch_shapes=[pltpu.VMEM((seq + max(POOL_WINDOWS), gw), F32)],
        compiler_params=_cparams(("arbitrary", "arbitrary")),
        name="multiscale_pool",
    )(z_rest, pool_w.astype(BF16), pool_scale.reshape(groups, 1, gw))


def _log_sigmoid(z):
    return jnp.minimum(z, 0.0) - jnp.log1p(jnp.exp(-jnp.abs(z)))


def _logaddexp(a, b):
    return jnp.maximum(a, b) + jnp.log1p(jnp.exp(-jnp.abs(a - b)))


def _hgrn_kernel(lbp_ref, ng_ref, q_ref, f_ref, i_ref, g_ref, o_ref, st_scr, b_scr, k_scr, v_scr,
                 *, layer, seq, hb):
    chunk, sub = HGRN_CHUNK, HGRN_SUB
    lbp = lbp_ref[...]
    e = jnp.exp(lbp - jnp.max(lbp, axis=0, keepdims=True))
    prob = e / jnp.sum(e, axis=0, keepdims=True)
    cs = prob[0:1]
    for l in range(1, layer + 1):
        cs = cs + prob[l:l + 1]
    lb = cs - prob[0:1]
    log_lb = jnp.log(jnp.maximum(lb, LB_FLOOR))
    log_1m = jnp.log1p(-lb)

    tril = (lax.broadcasted_iota(jnp.int32, (chunk, chunk), 0)
            >= lax.broadcasted_iota(jnp.int32, (chunk, chunk), 1)).astype(F32)
    trow = lax.broadcasted_iota(jnp.int32, (sub, 1), 0)
    st_scr[...] = jnp.zeros(st_scr.shape, F32)
    ng = ng_ref[...]

    def head_chunk(hh, rows):
        hs = slice(hh * HGRN_D, (hh + 1) * HGRN_D)
        logf = _logaddexp(log_lb[:, hs], log_1m[:, hs] + _log_sigmoid(f_ref[rows, hs]))
        b = _dot(tril, logf, precision=lax.Precision.HIGHEST)
        kk = 1.0 - jnp.exp(logf)
        qs = _silu(q_ref[rows, hs])
        v = i_ref[rows, hs]
        b_scr[hh] = b
        k_scr[hh] = kk
        v_scr[hh] = v
        st = st_scr[hh]
        vb = v.astype(BF16)
        o_inter = _dot_nt((qs * jnp.exp(b)).astype(BF16), st.astype(BF16))

        outs = []
        for blk in range(chunk // sub):
            s0 = blk * sub
            b_i = b[s0:s0 + sub]
            q_i = qs[s0:s0 + sub]
            o_i = o_inter[s0:s0 + sub]
            if blk > 0:
                b_ref0 = b[s0:s0 + 1]
                k_prev = kk[:s0] * jnp.exp(b_ref0 - b[:s0])
                a = _dot_nt((q_i * jnp.exp(b_i - b_ref0)).astype(BF16), k_prev.astype(BF16))
                o_i = o_i + _dot(a.astype(BF16), vb[:s0])
            for s in range(sub):
                b_s = b_scr[hh, s0 + s:s0 + s + 1, :]
                k_s = k_scr[hh, s0 + s:s0 + s + 1, :]
                v_s = v_scr[hh, s0 + s:s0 + s + 1, :]
                dec = jnp.exp(jnp.where(trow >= s, b_i - b_s, MASK_VALUE))
                colv = jnp.sum(q_i * dec * k_s, axis=1, keepdims=True)
                o_i = o_i + colv * v_s
            outs.append(o_i)
        o = jnp.concatenate(outs, axis=0)
        o_ref[rows, hs] = _rms(o, ng) * _silu(g_ref[rows, hs])

        b_last = b[chunk - 1:chunk]
        k_dec = kk * jnp.exp(b_last - b)
        st_scr[hh] = jnp.exp(b_last) * st + _dot_tn(vb, k_dec.astype(BF16))

    def chunk_body(c, carry):
        rows = pl.ds(pl.multiple_of(c * chunk, chunk), chunk)
        for hh in range(hb):
            head_chunk(hh, rows)
        return carry

    lax.fori_loop(0, seq // chunk, chunk_body, 0)


def _hgrn2(z_rest, lower_bounds, norm_g, layer, bsz, seq, d_model):
    t = z_rest.shape[0]
    heads = d_model // HGRN_D
    depth = lower_bounds.shape[0]
    hb = 4
    hw = hb * HGRN_D
    ng = heads // hb
    kern = functools.partial(_hgrn_kernel, layer=layer, seq=seq, hb=hb)

    def col(k):
        return pl.BlockSpec((seq, hw), lambda b, h, k=k: (b, k * ng + h))

    return pl.pallas_call(
        kern,
        grid=(bsz, ng),
        in_specs=[pl.BlockSpec((depth, hw), lambda b, h: (0, h)),
                  pl.BlockSpec((1, HGRN_D), lambda b, h: (0, 0)),
                  col(1), col(2), col(3), col(4)],
        out_specs=pl.BlockSpec((seq, hw), lambda b, h: (b, h)),
        out_shape=jax.ShapeDtypeStruct((t, d_model), F32),
        scratch_shapes=[pltpu.VMEM((hb, HGRN_D, HGRN_D), F32),
                        pltpu.VMEM((hb, HGRN_CHUNK, HGRN_D), F32),
                        pltpu.VMEM((hb, HGRN_CHUNK, HGRN_D), F32),
                        pltpu.VMEM((hb, HGRN_CHUNK, HGRN_D), F32)],
        compiler_params=_cparams(("arbitrary", "arbitrary")),
        name="hgrn2",
    )(lower_bounds, norm_g.reshape(1, HGRN_D), z_rest, z_rest, z_rest, z_rest)


def _merge_kernel(a_ref, p_ref, r_ref, ga_ref, gp_ref, gr_ref, x_ref, g1_ref, w_ref, o_ref):
    a = jnp.concatenate([a_ref[hh] for hh in range(a_ref.shape[0])], axis=1)
    merged = (jax.nn.sigmoid(ga_ref[...]) * a + jax.nn.sigmoid(gp_ref[...]) * p_ref[...]
              + jax.nn.sigmoid(gr_ref[...]) * r_ref[...])
    o_ref[...] = x_ref[...] + g1_ref[0] * _dot(merged.astype(BF16), w_ref[...])


def _merge_out(a_out, p_out, r_out, z_rest, x2, g1, w_out, seq):
    t, d = x2.shape
    tm = min(256, seq)
    row = pl.BlockSpec((tm, d), lambda i: (i, 0))

    def gate(k):
        return pl.BlockSpec((tm, d), lambda i, k=k: (i, 5 + k))

    return pl.pallas_call(
        _merge_kernel,
        grid=(t // tm,),
        in_specs=[pl.BlockSpec((d // ATT_DV, tm, ATT_DV), lambda i: (0, i, 0)), row, row,
                  gate(0), gate(1), gate(2), row,
                  pl.BlockSpec((1, 1, d), lambda i: ((i * tm) // seq, 0, 0)),
                  pl.BlockSpec((d, d), lambda i: (0, 0))],
        out_specs=row,
        out_shape=jax.ShapeDtypeStruct((t, d), F32),
        compiler_params=_cparams(("arbitrary",)),
        name="merge_out",
    )(a_out, p_out, r_out, z_rest, z_rest, z_rest, x2, g1, w_out.astype(BF16))


def _peer_query_kernel(x_ref, g_ref, sc_ref, sh_ref, wq_ref, keys_ref, h_ref, s_ref):
    h = (_rms(x_ref[...], g_ref[...]) * (1.0 + sc_ref[0]) + sh_ref[0]).astype(BF16)
    h_ref[...] = h
    dq = 2 * PEER_DHALF
    for hd in range(PEER_HEADS):
        q = _dot(h, wq_ref[:, hd * dq:(hd + 1) * dq]).astype(BF16)
        for p in range(2):
            s_ref[hd, p] = _dot_nt(keys_ref[p], q[:, p * PEER_DHALF:(p + 1) * PEER_DHALF])


def _peer_query(x2, g, sc, sh, w_query, sub_keys, seq):
    t, d = x2.shape
    tm = min(512, seq)
    nq = w_query.shape[1]
    return pl.pallas_call(
        _peer_query_kernel,
        grid=(t // tm,),
        in_specs=[pl.BlockSpec((tm, d), lambda i: (i, 0)),
                  pl.BlockSpec((1, d), lambda i: (0, 0)),
                  pl.BlockSpec((1, 1, d), lambda i: ((i * tm) // seq, 0, 0)),
                  pl.BlockSpec((1, 1, d), lambda i: ((i * tm) // seq, 0, 0)),
                  pl.BlockSpec((d, nq), lambda i: (0, 0)),
                  pl.BlockSpec((2, PEER_NKEYS, PEER_DHALF), lambda i: (0, 0, 0))],
        out_specs=[pl.BlockSpec((tm, d), lambda i: (i, 0)),
                   pl.BlockSpec((PEER_HEADS, 2, PEER_NKEYS, tm), lambda i: (0, 0, 0, i))],
        out_shape=[jax.ShapeDtypeStruct((t, d), BF16),
                   jax.ShapeDtypeStruct((PEER_HEADS, 2, PEER_NKEYS, t), F32)],
        compiler_params=_cparams(("arbitrary",)),
        name="peer_query",
    )(x2, g.reshape(1, d), sc, sh, w_query.astype(BF16), sub_keys.astype(BF16))


NEG_BIG = -3.0e38


def _top_values(s, n):
    vals = []
    cur = s
    for k in range(n):
        m = jnp.max(cur, axis=0, keepdims=True)
        vals.append(m)
        if k + 1 < n:
            cur = jnp.where(cur == m, NEG_BIG, cur)
    return vals


_PEER_N = PEER_TOPK + 1
_PEER_PAIRS = [(i, j) for i in range(_PEER_N) for j in range(_PEER_N) if (i + 1) * (j + 1) <= _PEER_N]
_PEER_CAND_ROWS = -(-len(_PEER_PAIRS) // 8) * 8


def _peer_select_kernel(s_ref, c_ref, g_ref, eb_ref, cand_scr, wts_scr):
    n = _PEER_N
    tt = cand_scr.shape[1]
    npairs = len(_PEER_PAIRS)
    cand_scr[npairs:, :] = jnp.full((_PEER_CAND_ROWS - npairs, tt), NEG_BIG, F32)
    wts_scr[npairs:, :] = jnp.zeros((_PEER_CAND_ROWS - npairs, tt), F32)
    for hd in range(PEER_HEADS):
        sa = s_ref[hd, 0]
        sb = s_ref[hd, 1]
        ta = _top_values(sa, n)
        tb = _top_values(sb, n)
        ea_top = [jnp.exp(v - ta[0]) for v in ta]
        eb_top = [jnp.exp(v - tb[0]) for v in tb]
        for k, (i, j) in enumerate(_PEER_PAIRS):
            cand_scr[k:k + 1, :] = ta[i] + tb[j]
            wts_scr[k:k + 1, :] = ea_top[i] * eb_top[j]
        cand = cand_scr[...]
        top = _top_values(cand, n)
        tau = 0.5 * (top[PEER_TOPK - 1] + top[PEER_TOPK])
        zsum = jnp.sum(jnp.where(cand > tau, wts_scr[...], 0.0), axis=0, keepdims=True)
        c_ref[hd] = tau - sa
        g_ref[hd] = jnp.exp(sa - ta[0]) / zsum
        eb_ref[hd] = jnp.exp(sb - tb[0])


def _peer_select(scores):
    heads, _, nk, t = scores.shape
    tt = min(256, t)
    spec = pl.BlockSpec((heads, nk, tt), lambda i: (0, 0, i))
    shape = jax.ShapeDtypeStruct((heads, nk, t), F32)
    return pl.pallas_call(
        _peer_select_kernel,
        grid=(t // tt,),
        in_specs=[pl.BlockSpec((heads, 2, nk, tt), lambda i: (0, 0, 0, i))],
        out_specs=[spec, spec, spec],
        out_shape=[shape, shape, shape],
        scratch_shapes=[pltpu.VMEM((_PEER_CAND_ROWS, tt), F32), pltpu.VMEM((_PEER_CAND_ROWS, tt), F32)],
        compiler_params=_cparams(("arbitrary",)),
        name="peer_select",
    )(scores)


def _peer_dense_kernel(h_ref, u_ref, vt_ref, sb_ref, eb_ref, c_ref, g_ref, x_ref, g2_ref, fg_ref,
                       o_ref, acc_scr, act_scr, p_scr, *, final_norm, rows_per_step, tt):
    e = pl.program_id(1)

    @pl.when(e == 0)
    def _():
        acc_scr[...] = jnp.zeros(acc_scr.shape, F32)

    act_scr[...] = _dot_nt(u_ref[...], h_ref[...])

    jg = 32

    def tile_body(ls, jt, carry):
        j0 = pl.multiple_of(jt * jg, jg)
        w = [jnp.zeros((jg, LANES), F32) for _ in range(rows_per_step)]
        for hd in range(PEER_HEADS):
            sbv = sb_ref[hd, pl.ds(j0, jg), ls]
            ebv = eb_ref[hd, pl.ds(j0, jg), ls]
            for r in range(rows_per_step):
                w[r] = w[r] + jnp.where(sbv > c_ref[hd, r:r + 1, ls], ebv * g_ref[hd, r:r + 1, ls], 0.0)
        for r in range(rows_per_step):
            rows = pl.ds(pl.multiple_of(r * PEER_NKEYS + j0, jg), jg)
            a = act_scr[rows, ls]
            gelu = 0.5 * a * (1.0 + lax.erf(a * (2.0 ** -0.5)))
            p_scr[rows, ls] = (w[r] * gelu).astype(BF16)
        return carry

    for lc in range(tt // LANES):
        ls = slice(lc * LANES, (lc + 1) * LANES)
        lax.fori_loop(0, PEER_NKEYS // jg, functools.partial(tile_body, ls), 0)
    acc_scr[...] += _dot(vt_ref[...], p_scr[...])

    @pl.when(e == pl.num_programs(1) - 1)
    def _():
        y = x_ref[...] + g2_ref[0] * acc_scr[...].T
        if final_norm:
            y = _rms(y, fg_ref[...])
        o_ref[...] = y


def _peer_dense(h_bf, u_tab, v_tab, scores, c_arr, g_arr, eb_arr, x2, g2, final_g, final_norm, seq):
    t, d = x2.shape
    n_exp = u_tab.shape[0]
    heads, _, nk, _ = scores.shape
    tt = min(512, seq)
    rows_per_step = 8
    et = rows_per_step * nk
    vt_tiles = v_tab.astype(BF16).reshape(n_exp // et, et, d).transpose(0, 2, 1)
    kern = functools.partial(_peer_dense_kernel, final_norm=final_norm,
                             rows_per_step=rows_per_step, tt=tt)
    return pl.pallas_call(
        kern,
        grid=(t // tt, n_exp // et),
        in_specs=[pl.BlockSpec((tt, d), lambda i, e: (i, 0)),
                  pl.BlockSpec((et, d), lambda i, e: (e, 0)),
                  pl.BlockSpec((None, d, et), lambda i, e: (e, 0, 0)),
                  pl.BlockSpec((heads, None, nk, tt), lambda i, e: (0, 1, 0, i)),
                  pl.BlockSpec((heads, nk, tt), lambda i, e: (0, 0, i)),
                  pl.BlockSpec((heads, rows_per_step, tt), lambda i, e: (0, e, i)),
                  pl.BlockSpec((heads, rows_per_step, tt), lambda i, e: (0, e, i)),
                  pl.BlockSpec((tt, d), lambda i, e: (i, 0)),
                  pl.BlockSpec((1, 1, d), lambda i, e: ((i * tt) // seq, 0, 0)),
                  pl.BlockSpec((1, d), lambda i, e: (0, 0))],
        out_specs=pl.BlockSpec((tt, d), lambda i, e: (i, 0)),
        out_shape=jax.ShapeDtypeStruct((t, d), F32),
        scratch_shapes=[pltpu.VMEM((d, tt), F32), pltpu.VMEM((et, tt), F32),
                        pltpu.VMEM((et, tt), BF16)],
        compiler_params=_cparams(("arbitrary", "arbitrary")),
        name="peer_dense",
    )(h_bf, u_tab.astype(BF16), vt_tiles, scores, eb_arr, c_arr, g_arr, x2, g2,
      final_g.reshape(1, d))


def kernel(x, c, ada_w, ada_b, norm_mix_g, w_in, diff_lambda, diff_subln_g, pool_w, pool_scale,
           hgrn_lower_bounds, hgrn_norm_g, w_out, norm_ffn_g, peer_w_query, peer_sub_keys,
           peer_u, peer_v, final_g):
    bsz, seq, d = x.shape
    depth = ada_w.shape[0]
    att_cols = 3 * d
    x2 = x.reshape(bsz * seq, d)
    ada = _ada(c, ada_w, ada_b)
    for l in range(depth):
        sh1, sc1, g1, sh2, sc2, g2 = [ada[l, :, None, k * d:(k + 1) * d] for k in range(6)]
        w_l = w_in[l].astype(BF16)
        z_att = _norm_mod_matmul(x2, norm_mix_g[l], sc1, sh1, w_l[:, :att_cols], BF16, seq, head_major=True)
        z_rest = _norm_mod_matmul(x2, norm_mix_g[l], sc1, sh1, w_l[:, att_cols:], F32, seq)
        a_out = _diff_attention(z_att, diff_lambda[l], diff_subln_g[l], l, bsz, seq)
        p_out = _multiscale_pool(z_rest, pool_w[l], pool_scale[l], bsz, seq)
        r_out = _hgrn2(z_rest, hgrn_lower_bounds, hgrn_norm_g[l], l, bsz, seq, d)
        x2 = _merge_out(a_out, p_out, r_out, z_rest, x2, g1, w_out[l], seq)
        h_bf, scores = _peer_query(x2, norm_ffn_g[l], sc2, sh2, peer_w_query[l], peer_sub_keys[l], seq)
        c_arr, g_arr, eb_arr = _peer_select(scores)
        x2 = _peer_dense(h_bf, peer_u[l], peer_v[l], scores, c_arr, g_arr, eb_arr, x2, g2,
                         final_g, l == depth - 1, seq)
    return x2.reshape(bsz, seq, d)
```

```python
import functools
import math

import jax
import jax.numpy as jnp
from jax import lax
from jax.experimental import pallas as pl
from jax.experimental.pallas import tpu as pltpu

F32 = jnp.float32
BF16 = jnp.bfloat16

RMS_EPS = 1e-6
MASK_VALUE = -1e30
LB_FLOOR = 1e-20
LANES = 128
BF16_ROWS = 16

ATT_DK = 64
ATT_DV = 2 * ATT_DK
POOL_WINDOWS = (2, 4, 8, 16)
HGRN_D = 128
HGRN_CHUNK = 64
HGRN_SUB = 16
PEER_HEADS = 8
PEER_NKEYS = 128
PEER_TOPK = 16
PEER_DHALF = 128

VMEM_LIMIT = 56 * 1024 * 1024


def _cparams(sem):
    return pltpu.CompilerParams(dimension_semantics=sem, vmem_limit_bytes=VMEM_LIMIT)


def _dot(a, b, **kw):
    return jnp.dot(a, b, preferred_element_type=F32, **kw)


def _dot_nt(a, b):
    return lax.dot_general(a, b, (((1,), (1,)), ((), ())), preferred_element_type=F32)


def _dot_tn(a, b):
    return lax.dot_general(a, b, (((0,), (0,)), ((), ())), preferred_element_type=F32)


def _rms(x, g):
    return x * lax.rsqrt(jnp.mean(x * x, axis=-1, keepdims=True) + RMS_EPS) * g


def _silu(x):
    return x * jax.nn.sigmoid(x)


def _ada_kernel(c_ref, w_ref, b_ref, o_ref):
    o_ref[0] = _dot(_silu(c_ref[...]), w_ref[0], precision=lax.Precision.HIGHEST) + b_ref[0]


def _ada(c, ada_w, ada_b):
    depth, d, n = ada_w.shape
    bsz = c.shape[0]
    tn = 1024
    return pl.pallas_call(
        _ada_kernel,
        grid=(depth, n // tn),
        in_specs=[pl.BlockSpec((bsz, d), lambda l, j: (0, 0)),
                  pl.BlockSpec((1, d, tn), lambda l, j: (l, 0, j)),
                  pl.BlockSpec((1, 1, tn), lambda l, j: (l, 0, j))],
        out_specs=pl.BlockSpec((1, bsz, tn), lambda l, j: (l, 0, j)),
        out_shape=jax.ShapeDtypeStruct((depth, bsz, n), F32),
        compiler_params=_cparams(("arbitrary", "arbitrary")),
        name="ada_proj",
    )(c, ada_w, ada_b.reshape(depth, 1, n))


def _norm_mod_matmul_kernel(x_ref, g_ref, sc_ref, sh_ref, w_ref, o_ref, h_scr):
    @pl.when(pl.program_id(1) == 0)
    def _():
        h = _rms(x_ref[...], g_ref[...]) * (1.0 + sc_ref[0]) + sh_ref[0]
        h_scr[...] = h.astype(BF16)

    res = _dot(h_scr[...], w_ref[...]).astype(o_ref.dtype)
    if len(o_ref.shape) == 2:
        o_ref[...] = res
    else:
        for hh in range(o_ref.shape[0]):
            o_ref[hh] = res[:, hh * LANES:(hh + 1) * LANES]


def _norm_mod_matmul(x2, g, sc, sh, w, out_dtype, seq, head_major=False):
    t, d = x2.shape
    n = w.shape[1]
    tm = min(1024, seq)
    tn = 1024
    if head_major:
        out_spec = pl.BlockSpec((tn // LANES, tm, LANES), lambda i, j: (j, i, 0))
        out_shape = jax.ShapeDtypeStruct((n // LANES, t, LANES), out_dtype)
    else:
        out_spec = pl.BlockSpec((tm, tn), lambda i, j: (i, j))
        out_shape = jax.ShapeDtypeStruct((t, n), out_dtype)
    return pl.pallas_call(
        _norm_mod_matmul_kernel,
        grid=(t // tm, n // tn),
        in_specs=[pl.BlockSpec((tm, d), lambda i, j: (i, 0)),
                  pl.BlockSpec((1, d), lambda i, j: (0, 0)),
                  pl.BlockSpec((1, 1, d), lambda i, j: ((i * tm) // seq, 0, 0)),
                  pl.BlockSpec((1, 1, d), lambda i, j: ((i * tm) // seq, 0, 0)),
                  pl.BlockSpec((d, tn), lambda i, j: (0, j))],
        out_specs=out_spec,
        out_shape=out_shape,
        scratch_shapes=[pltpu.VMEM((tm, d), BF16)],
        compiler_params=_cparams(("arbitrary", "arbitrary")),
        name="norm_mod_matmul",
    )(x2, g.reshape(1, d), sc, sh, w)


LOG2E = 1.4426950408889634


def _attn_kernel(lam_ref, g_ref, slope_ref, q_ref, k_ref, v_ref, o_ref, s_scr, mx_scr, ls_scr, acc_scr,
                 *, lam_init, tq):
    qi = pl.program_id(2)
    slope = slope_ref[0] * LOG2E
    lane = lax.broadcasted_iota(jnp.int32, (1, 2 * ATT_DK), 1)
    q = (q_ref[...].astype(F32) * (ATT_DK ** -0.5 * LOG2E)).astype(BF16)
    zero = jnp.zeros_like(q)
    qq = jnp.concatenate([jnp.where(lane < ATT_DK, q, zero), jnp.where(lane >= ATT_DK, q, zero)], axis=0)
    col = lax.broadcasted_iota(jnp.int32, (1, tq), 1)

    def logits(j):
        k = k_ref[pl.ds(pl.multiple_of(j * tq, tq), tq), :]
        bias = slope * (col + (j - qi) * tq).astype(F32)
        return _dot_nt(qq, k) + bias

    def fold_max(s):
        m = s[:, :LANES]
        for c in range(1, tq // LANES):
            m = jnp.maximum(m, s[:, c * LANES:(c + 1) * LANES])
        mx_scr[...] = jnp.maximum(mx_scr[...], m)

    mx_scr[...] = jnp.full(mx_scr.shape, MASK_VALUE, F32)

    def pass_a(j, carry):
        s = logits(j)
        s_scr[j] = s
        fold_max(s)
        return carry

    lax.fori_loop(0, qi, pass_a, 0)
    row = lax.broadcasted_iota(jnp.int32, (2 * tq, tq), 0) & (tq - 1)
    keep = row >= lax.broadcasted_iota(jnp.int32, (2 * tq, tq), 1)
    s = jnp.where(keep, logits(qi), MASK_VALUE)
    s_scr[qi] = s
    fold_max(s)

    m_row = jnp.max(mx_scr[...], axis=1, keepdims=True)
    mx_scr[...] = jnp.broadcast_to(m_row, mx_scr.shape)
    ls_scr[...] = jnp.zeros(ls_scr.shape, F32)
    acc_scr[...] = jnp.zeros(acc_scr.shape, F32)

    def pass_b(j, carry):
        s = s_scr[j]
        m = mx_scr[...]
        ps = [jnp.exp2(s[:, c * LANES:(c + 1) * LANES] - m) for c in range(tq // LANES)]
        tot = ps[0]
        for p in ps[1:]:
            tot = tot + p
        ls_scr[...] += tot
        p = jnp.concatenate(ps, axis=1).astype(BF16)
        v = v_ref[pl.ds(pl.multiple_of(j * tq, tq), tq), :]
        acc_scr[...] += _dot(p, v)
        return carry

    lax.fori_loop(0, qi + 1, pass_b, 0)

    lq = lam_ref[...]
    lam = (jnp.exp(jnp.sum(lq[0:1] * lq[1:2], axis=1, keepdims=True))
           - jnp.exp(jnp.sum(lq[2:3] * lq[3:4], axis=1, keepdims=True)) + lam_init)
    o2 = acc_scr[...] / jnp.sum(ls_scr[...], axis=1, keepdims=True)
    o = o2[:tq] - lam * o2[tq:]
    o_ref[...] = _rms(o, g_ref[...]) * (1.0 - lam_init)


def _diff_attention(z_att, lam_qk, subln_g, layer, bsz, seq):
    t = z_att.shape[1]
    heads = z_att.shape[0] // 3
    tq = min(512, seq)
    nq = seq // tq
    lam_init = 0.8 - 0.6 * math.exp(-0.3 * layer)
    slopes = jnp.asarray([2.0 ** (-8.0 * (h + 1) / heads) for h in range(heads)], F32)
    slopes = jnp.broadcast_to(slopes[:, None, None], (heads, 1, tq))
    kern = functools.partial(_attn_kernel, lam_init=lam_init, tq=tq)
    return pl.pallas_call(
        kern,
        grid=(bsz, heads, nq),
        in_specs=[pl.BlockSpec((4, ATT_DK), lambda b, h, i: (0, 0)),
                  pl.BlockSpec((1, ATT_DV), lambda b, h, i: (0, 0)),
                  pl.BlockSpec((1, 1, tq), lambda b, h, i: (h, 0, 0)),
                  pl.BlockSpec((None, tq, ATT_DV), lambda b, h, i: (h, b * nq + i, 0)),
                  pl.BlockSpec((None, seq, ATT_DV), lambda b, h, i: (heads + h, b, 0)),
                  pl.BlockSpec((None, seq, ATT_DV), lambda b, h, i: (2 * heads + h, b, 0))],
        out_specs=pl.BlockSpec((None, tq, ATT_DV), lambda b, h, i: (h, b * nq + i, 0)),
        out_shape=jax.ShapeDtypeStruct((heads, t, ATT_DV), F32),
        scratch_shapes=[pltpu.VMEM((nq, 2 * tq, tq), F32), pltpu.VMEM((2 * tq, LANES), F32),
                        pltpu.VMEM((2 * tq, LANES), F32), pltpu.VMEM((2 * tq, ATT_DV), F32)],
        compiler_params=_cparams(("arbitrary", "arbitrary", "arbitrary")),
        name="diff_attention",
    )(lam_qk, subln_g.reshape(1, ATT_DV), slopes, z_att, z_att, z_att)


def _pool_kernel(p_ref, w_ref, sc_ref, o_ref, pad_scr, *, seq):
    g = pl.program_id(1)
    pad = max(POOL_WINDOWS)
    p = p_ref[...]
    pad_scr[0:pad, :] = jnp.zeros((pad, p.shape[1]), F32)
    pad_scr[pad:pad + seq, :] = p
    t1 = lax.broadcasted_iota(jnp.int32, (seq, 1), 0) + 1

    for gi, win in enumerate(POOL_WINDOWS):
        @pl.when(g == gi)
        def _(win=win):
            acc = p
            for s in range(1, win):
                acc = acc + pad_scr[pad - s:pad - s + seq, :]
            count = jnp.minimum(t1, win).astype(F32)
            pooled = acc / count - p
            y = _dot(pooled.astype(BF16), w_ref[0])
            o_ref[...] = y * sc_ref[0]


def _multiscale_pool(z_rest, pool_w, pool_scale, bsz, seq):
    t = z_rest.shape[0]
    groups, gw, _ = pool_w.shape
    kern = functools.partial(_pool_kernel, seq=seq)
    return pl.pallas_call(
        kern,
        grid=(bsz, groups),
        in_specs=[pl.BlockSpec((seq, gw), lambda b, g: (b, g)),
                  pl.BlockSpec((1, gw, gw), lambda b, g: (g, 0, 0)),
                  pl.BlockSpec((1, 1, gw), lambda b, g: (g, 0, 0))],
        out_specs=pl.BlockSpec((seq, gw), lambda b, g: (b, g)),
        out_shape=jax.ShapeDtypeStruct((t, groups * gw), F32),
        scratch_shapes=[pltpu.VMEM((seq + max(POOL_WINDOWS), gw), F32)],
        compiler_params=_cparams(("arbitrary", "arbitrary")),
        name="multiscale_pool",
    )(z_rest, pool_w.astype(BF16), pool_scale.reshape(groups, 1, gw))


def _log_sigmoid(z):
    return jnp.minimum(z, 0.0) - jnp.log1p(jnp.exp(-jnp.abs(z)))


def _logaddexp(a, b):
    return jnp.maximum(a, b) + jnp.log1p(jnp.exp(-jnp.abs(a - b)))


def _hgrn_kernel(lbp_ref, ng_ref, q_ref, f_ref, i_ref, g_ref, o_ref, st_scr, b_scr, k_scr, v_scr,
                 *, layer, seq, hb):
    chunk, sub = HGRN_CHUNK, HGRN_SUB
    lbp = lbp_ref[...]
    e = jnp.exp(lbp - jnp.max(lbp, axis=0, keepdims=True))
    prob = e / jnp.sum(e, axis=0, keepdims=True)
    cs = prob[0:1]
    for l in range(1, layer + 1):
        cs = cs + prob[l:l + 1]
    lb = cs - prob[0:1]
    log_lb = jnp.log(jnp.maximum(lb, LB_FLOOR))
    log_1m = jnp.log1p(-lb)

    tril = (lax.broadcasted_iota(jnp.int32, (chunk, chunk), 0)
            >= lax.broadcasted_iota(jnp.int32, (chunk, chunk), 1)).astype(F32)
    trow = lax.broadcasted_iota(jnp.int32, (sub, 1), 0)
    st_scr[...] = jnp.zeros(st_scr.shape, F32)
    ng = ng_ref[...]

    def head_chunk(hh, rows):
        hs = slice(hh * HGRN_D, (hh + 1) * HGRN_D)
        logf = _logaddexp(log_lb[:, hs], log_1m[:, hs] + _log_sigmoid(f_ref[rows, hs]))
        b = _dot(tril, logf, precision=lax.Precision.HIGHEST)
        kk = 1.0 - jnp.exp(logf)
        qs = _silu(q_ref[rows, hs])
        v = i_ref[rows, hs]
        b_scr[hh] = b
        k_scr[hh] = kk
        v_scr[hh] = v
        st = st_scr[hh]
        vb = v.astype(BF16)
        o_inter = _dot_nt((qs * jnp.exp(b)).astype(BF16), st.astype(BF16))

        outs = []
        for blk in range(chunk // sub):
            s0 = blk * sub
            b_i = b[s0:s0 + sub]
            q_i = qs[s0:s0 + sub]
            o_i = o_inter[s0:s0 + sub]
            if blk > 0:
                b_ref0 = b[s0:s0 + 1]
                k_prev = kk[:s0] * jnp.exp(b_ref0 - b[:s0])
                a = _dot_nt((q_i * jnp.exp(b_i - b_ref0)).astype(BF16), k_prev.astype(BF16))
                o_i = o_i + _dot(a.astype(BF16), vb[:s0])
            for s in range(sub):
                b_s = b_scr[hh, s0 + s:s0 + s + 1, :]
                k_s = k_scr[hh, s0 + s:s0 + s + 1, :]
                v_s = v_scr[hh, s0 + s:s0 + s + 1, :]
                dec = jnp.exp(jnp.where(trow >= s, b_i - b_s, MASK_VALUE))
                colv = jnp.sum(q_i * dec * k_s, axis=1, keepdims=True)
                o_i = o_i + colv * v_s
            outs.append(o_i)
        o = jnp.concatenate(outs, axis=0)
        o_ref[rows, hs] = _rms(o, ng) * _silu(g_ref[rows, hs])

        b_last = b[chunk - 1:chunk]
        k_dec = kk * jnp.exp(b_last - b)
        st_scr[hh] = jnp.exp(b_last) * st + _dot_tn(vb, k_dec.astype(BF16))

    def chunk_body(c, carry):
        rows = pl.ds(pl.multiple_of(c * chunk, chunk), chunk)
        for hh in range(hb):
            head_chunk(hh, rows)
        return carry

    lax.fori_loop(0, seq // chunk, chunk_body, 0)


def _hgrn2(z_rest, lower_bounds, norm_g, layer, bsz, seq, d_model):
    t = z_rest.shape[0]
    heads = d_model // HGRN_D
    depth = lower_bounds.shape[0]
    hb = 4
    hw = hb * HGRN_D
    ng = heads // hb
    kern = functools.partial(_hgrn_kernel, layer=layer, seq=seq, hb=hb)

    def col(k):
        return pl.BlockSpec((seq, hw), lambda b, h, k=k: (b, k * ng + h))

    return pl.pallas_call(
        kern,
        grid=(bsz, ng),
        in_specs=[pl.BlockSpec((depth, hw), lambda b, h: (0, h)),
                  pl.BlockSpec((1, HGRN_D), lambda b, h: (0, 0)),
                  col(1), col(2), col(3), col(4)],
        out_specs=pl.BlockSpec((seq, hw), lambda b, h: (b, h)),
        out_shape=jax.ShapeDtypeStruct((t, d_model), F32),
        scratch_shapes=[pltpu.VMEM((hb, HGRN_D, HGRN_D), F32),
                        pltpu.VMEM((hb, HGRN_CHUNK, HGRN_D), F32),
                        pltpu.VMEM((hb, HGRN_CHUNK, HGRN_D), F32),
                        pltpu.VMEM((hb, HGRN_CHUNK, HGRN_D), F32)],
        compiler_params=_cparams(("arbitrary", "arbitrary")),
        name="hgrn2",
    )(lower_bounds, norm_g.reshape(1, HGRN_D), z_rest, z_rest, z_rest, z_rest)


def _merge_kernel(a_ref, p_ref, r_ref, ga_ref, gp_ref, gr_ref, x_ref, g1_ref, w_ref, o_ref):
    a = jnp.concatenate([a_ref[hh] for hh in range(a_ref.shape[0])], axis=1)
    merged = (jax.nn.sigmoid(ga_ref[...]) * a + jax.nn.sigmoid(gp_ref[...]) * p_ref[...]
              + jax.nn.sigmoid(gr_ref[...]) * r_ref[...])
    o_ref[...] = x_ref[...] + g1_ref[0] * _dot(merged.astype(BF16), w_ref[...])


def _merge_out(a_out, p_out, r_out, z_rest, x2, g1, w_out, seq):
    t, d = x2.shape
    tm = min(256, seq)
    row = pl.BlockSpec((tm, d), lambda i: (i, 0))

    def gate(k):
        return pl.BlockSpec((tm, d), lambda i, k=k: (i, 5 + k))

    return pl.pallas_call(
        _merge_kernel,
        grid=(t // tm,),
        in_specs=[pl.BlockSpec((d // ATT_DV, tm, ATT_DV), lambda i: (0, i, 0)), row, row,
                  gate(0), gate(1), gate(2), row,
                  pl.BlockSpec((1, 1, d), lambda i: ((i * tm) // seq, 0, 0)),
                  pl.BlockSpec((d, d), lambda i: (0, 0))],
        out_specs=row,
        out_shape=jax.ShapeDtypeStruct((t, d), F32),
        compiler_params=_cparams(("arbitrary",)),
        name="merge_out",
    )(a_out, p_out, r_out, z_rest, z_rest, z_rest, x2, g1, w_out.astype(BF16))


def _peer_query_kernel(x_ref, g_ref, sc_ref, sh_ref, wq_ref, keys_ref, h_ref, s_ref):
    h = (_rms(x_ref[...], g_ref[...]) * (1.0 + sc_ref[0]) + sh_ref[0]).astype(BF16)
    h_ref[...] = h
    dq = 2 * PEER_DHALF
    for hd in range(PEER_HEADS):
        q = _dot(h, wq_ref[:, hd * dq:(hd + 1) * dq]).astype(BF16)
        for p in range(2):
            s_ref[hd, p] = _dot_nt(keys_ref[p], q[:, p * PEER_DHALF:(p + 1) * PEER_DHALF])


def _peer_query(x2, g, sc, sh, w_query, sub_keys, seq):
    t, d = x2.shape
    tm = min(512, seq)
    nq = w_query.shape[1]
    return pl.pallas_call(
        _peer_query_kernel,
        grid=(t // tm,),
        in_specs=[pl.BlockSpec((tm, d), lambda i: (i, 0)),
                  pl.BlockSpec((1, d), lambda i: (0, 0)),
                  pl.BlockSpec((1, 1, d), lambda i: ((i * tm) // seq, 0, 0)),
                  pl.BlockSpec((1, 1, d), lambda i: ((i * tm) // seq, 0, 0)),
                  pl.BlockSpec((d, nq), lambda i: (0, 0)),
                  pl.BlockSpec((2, PEER_NKEYS, PEER_DHALF), lambda i: (0, 0, 0))],
        out_specs=[pl.BlockSpec((tm, d), lambda i: (i, 0)),
                   pl.BlockSpec((PEER_HEADS, 2, PEER_NKEYS, tm), lambda i: (0, 0, 0, i))],
        out_shape=[jax.ShapeDtypeStruct((t, d), BF16),
                   jax.ShapeDtypeStruct((PEER_HEADS, 2, PEER_NKEYS, t), F32)],
        compiler_params=_cparams(("arbitrary",)),
        name="peer_query",
    )(x2, g.reshape(1, d), sc, sh, w_query.astype(BF16), sub_keys.astype(BF16))


NEG_BIG = -3.0e38


def _top_values(s, n):
    vals = []
    cur = s
    rank = jnp.full(s.shape, float(n), F32)
    for k in range(n):
        m = jnp.max(cur, axis=0, keepdims=True)
        vals.append(m)
        hit = cur == m
        rank = jnp.where(hit, float(k), rank)
        if k + 1 < n:
            cur = jnp.where(hit, NEG_BIG, cur)
    return vals, rank


_PEER_N = PEER_TOPK + 1
_PEER_PAIRS = [(i, j) for i in range(_PEER_N) for j in range(_PEER_N) if (i + 1) * (j + 1) <= _PEER_N]
_PEER_CAND_ROWS = -(-len(_PEER_PAIRS) // 8) * 8


def _peer_select_kernel(s_ref, n_ref, g_ref, rb_ref, eb_ref, cand_scr, wts_scr):
    n = _PEER_N
    tt = cand_scr.shape[1]
    npairs = len(_PEER_PAIRS)
    cand_scr[npairs:, :] = jnp.full((_PEER_CAND_ROWS - npairs, tt), NEG_BIG, F32)
    wts_scr[npairs:, :] = jnp.zeros((_PEER_CAND_ROWS - npairs, tt), F32)
    for hd in range(PEER_HEADS):
        sa = s_ref[hd, 0]
        sb = s_ref[hd, 1]
        ta, _ = _top_values(sa, n)
        tb, rank_b = _top_values(sb, n)
        ea_top = [jnp.exp(v - ta[0]) for v in ta]
        eb_top = [jnp.exp(v - tb[0]) for v in tb]
        for k, (i, j) in enumerate(_PEER_PAIRS):
            cand_scr[k:k + 1, :] = ta[i] + tb[j]
            wts_scr[k:k + 1, :] = ea_top[i] * eb_top[j]
        cand = cand_scr[...]
        top, _ = _top_values(cand, n)
        tau = 0.5 * (top[PEER_TOPK - 1] + top[PEER_TOPK])
        zsum = jnp.sum(jnp.where(cand > tau, wts_scr[...], 0.0), axis=0, keepdims=True)
        count = jnp.zeros(sa.shape, F32)
        for k in range(n):
            count = jnp.where(sa > tau - tb[k], float(k + 1), count)
        n_ref[hd] = count
        g_ref[hd] = jnp.exp(sa - ta[0]) / zsum
        rb_ref[hd * PEER_NKEYS:(hd + 1) * PEER_NKEYS, :] = rank_b.astype(BF16)
        eb_ref[hd * PEER_NKEYS:(hd + 1) * PEER_NKEYS, :] = jnp.exp(sb - tb[0]).astype(BF16)


def _peer_select(scores):
    heads, _, nk, t = scores.shape
    tt = min(256, t)
    spec = pl.BlockSpec((heads, nk, tt), lambda i: (0, 0, i))
    f32 = jax.ShapeDtypeStruct((heads, nk, t), F32)
    spec2 = pl.BlockSpec((heads * nk, tt), lambda i: (0, i))
    bf16 = jax.ShapeDtypeStruct((heads * nk, t), BF16)
    return pl.pallas_call(
        _peer_select_kernel,
        grid=(t // tt,),
        in_specs=[pl.BlockSpec((heads, 2, nk, tt), lambda i: (0, 0, 0, i))],
        out_specs=[spec, spec, spec2, spec2],
        out_shape=[f32, f32, bf16, bf16],
        scratch_shapes=[pltpu.VMEM((_PEER_CAND_ROWS, tt), F32), pltpu.VMEM((_PEER_CAND_ROWS, tt), F32)],
        compiler_params=_cparams(("arbitrary",)),
        name="peer_select",
    )(scores)


def _peer_dense_kernel(h_ref, u_ref, vt_ref, rb_ref, eb_ref, n_ref, g_ref, x_ref, g2_ref, fg_ref,
                       o_ref, acc_scr, act_scr, p_scr, bc_scr, rb_scr, eb_scr, *, final_norm, rows_per_step, tt):
    e = pl.program_id(1)

    @pl.when(e == 0)
    def _():
        acc_scr[...] = jnp.zeros(acc_scr.shape, F32)
        rb_scr[...] = rb_ref[...]
        eb_scr[...] = eb_ref[...]

    act_scr[...] = _dot_nt(u_ref[...], h_ref[...])

    pack = bc_scr.shape[3]
    for hd in range(PEER_HEADS):
        for r in range(rows_per_step):
            bc_scr[0, hd, r] = jnp.broadcast_to(n_ref[hd, r:r + 1, :], (pack, tt)).astype(BF16)
            bc_scr[1, hd, r] = jnp.broadcast_to(g_ref[hd, r:r + 1, :], (pack, tt)).astype(BF16)

    jg = 2 * pack
    zero = jnp.zeros((jg, LANES), BF16)

    def tile_body(ls, jt, carry):
        j0 = pl.multiple_of(jt * jg, jg)
        w = [zero for _ in range(rows_per_step)]
        for hd in range(PEER_HEADS):
            rbv = rb_scr[pl.ds(hd * PEER_NKEYS + j0, jg), ls]
            ebv = eb_scr[pl.ds(hd * PEER_NKEYS + j0, jg), ls]
            for r in range(rows_per_step):
                cnt = bc_scr[0, hd, r, :, ls]
                gate = bc_scr[1, hd, r, :, ls]
                cnt = jnp.concatenate([cnt, cnt], axis=0)
                gate = jnp.concatenate([gate, gate], axis=0)
                w[r] = w[r] + jnp.where(rbv < cnt, ebv * gate, zero)
        for r in range(rows_per_step):
            rows = pl.ds(pl.multiple_of(r * PEER_NKEYS + j0, jg), jg)
            a = act_scr[rows, ls]
            gelu = 0.5 * a * (1.0 + lax.erf(a * (2.0 ** -0.5)))
            p_scr[rows, ls] = w[r] * gelu.astype(BF16)
        return carry

    for lc in range(tt // LANES):
        ls = slice(lc * LANES, (lc + 1) * LANES)
        lax.fori_loop(0, PEER_NKEYS // jg, functools.partial(tile_body, ls), 0)
    acc_scr[...] += _dot(vt_ref[...], p_scr[...])

    @pl.when(e == pl.num_programs(1) - 1)
    def _():
        y = x_ref[...] + g2_ref[0] * acc_scr[...].T
        if final_norm:
            y = _rms(y, fg_ref[...])
        o_ref[...] = y


def _peer_dense(h_bf, u_tab, v_tab, n_arr, g_arr, rb_arr, eb_arr, x2, g2, final_g, final_norm, seq):
    t, d = x2.shape
    n_exp = u_tab.shape[0]
    heads, nk, _ = n_arr.shape
    tt = min(512, seq)
    rows_per_step = 8
    et = rows_per_step * nk
    vt_tiles = v_tab.astype(BF16).reshape(n_exp // et, et, d).transpose(0, 2, 1)
    kern = functools.partial(_peer_dense_kernel, final_norm=final_norm,
                             rows_per_step=rows_per_step, tt=tt)
    return pl.pallas_call(
        kern,
        grid=(t // tt, n_exp // et),
        in_specs=[pl.BlockSpec((tt, d), lambda i, e: (i, 0)),
                  pl.BlockSpec((et, d), lambda i, e: (e, 0)),
                  pl.BlockSpec((None, d, et), lambda i, e: (e, 0, 0)),
                  pl.BlockSpec((heads * nk, tt), lambda i, e: (0, i)),
                  pl.BlockSpec((heads * nk, tt), lambda i, e: (0, i)),
                  pl.BlockSpec((heads, rows_per_step, tt), lambda i, e: (0, e, i)),
                  pl.BlockSpec((heads, rows_per_step, tt), lambda i, e: (0, e, i)),
                  pl.BlockSpec((tt, d), lambda i, e: (i, 0)),
                  pl.BlockSpec((1, 1, d), lambda i, e: ((i * tt) // seq, 0, 0)),
                  pl.BlockSpec((1, d), lambda i, e: (0, 0))],
        out_specs=pl.BlockSpec((tt, d), lambda i, e: (i, 0)),
        out_shape=jax.ShapeDtypeStruct((t, d), F32),
        scratch_shapes=[pltpu.VMEM((d, tt), F32), pltpu.VMEM((et, tt), F32), pltpu.VMEM((et, tt), BF16),
                        pltpu.VMEM((2, heads, rows_per_step, BF16_ROWS, tt), BF16),
                        pltpu.VMEM((heads * nk, tt), BF16), pltpu.VMEM((heads * nk, tt), BF16)],
        compiler_params=_cparams(("arbitrary", "arbitrary")),
        name="peer_dense",
    )(h_bf, u_tab.astype(BF16), vt_tiles, rb_arr, eb_arr, n_arr, g_arr, x2, g2,
      final_g.reshape(1, d))


def kernel(x, c, ada_w, ada_b, norm_mix_g, w_in, diff_lambda, diff_subln_g, pool_w, pool_scale,
           hgrn_lower_bounds, hgrn_norm_g, w_out, norm_ffn_g, peer_w_query, peer_sub_keys,
           peer_u, peer_v, final_g):
    bsz, seq, d = x.shape
    depth = ada_w.shape[0]
    att_cols = 3 * d
    x2 = x.reshape(bsz * seq, d)
    ada = _ada(c, ada_w, ada_b)
    for l in range(depth):
        sh1, sc1, g1, sh2, sc2, g2 = [ada[l, :, None, k * d:(k + 1) * d] for k in range(6)]
        w_l = w_in[l].astype(BF16)
        z_att = _norm_mod_matmul(x2, norm_mix_g[l], sc1, sh1, w_l[:, :att_cols], BF16, seq, head_major=True)
        z_rest = _norm_mod_matmul(x2, norm_mix_g[l], sc1, sh1, w_l[:, att_cols:], F32, seq)
        a_out = _diff_attention(z_att, diff_lambda[l], diff_subln_g[l], l, bsz, seq)
        p_out = _multiscale_pool(z_rest, pool_w[l], pool_scale[l], bsz, seq)
        r_out = _hgrn2(z_rest, hgrn_lower_bounds, hgrn_norm_g[l], l, bsz, seq, d)
        x2 = _merge_out(a_out, p_out, r_out, z_rest, x2, g1, w_out[l], seq)
        h_bf, scores = _peer_query(x2, norm_ffn_g[l], sc2, sh2, peer_w_query[l], peer_sub_keys[l], seq)
        n_arr, g_arr, rb_arr, eb_arr = _peer_select(scores)
        x2 = _peer_dense(h_bf, peer_u[l], peer_v[l], n_arr, g_arr, rb_arr, eb_arr, x2, g2,
                         final_g, l == depth - 1, seq)
    return x2.reshape(bsz, seq, d)
```

```python
import functools
import math

import jax
import jax.numpy as jnp
from jax import lax
from jax.experimental import pallas as pl
from jax.experimental.pallas import tpu as pltpu

F32 = jnp.float32
BF16 = jnp.bfloat16

RMS_EPS = 1e-6
MASK_VALUE = -1e30
LB_FLOOR = 1e-20
LANES = 128

ATT_DK = 64
ATT_DV = 2 * ATT_DK
POOL_WINDOWS = (2, 4, 8, 16)
HGRN_D = 128
HGRN_CHUNK = 64
HGRN_SUB = 16
PEER_HEADS = 8
PEER_NKEYS = 128
PEER_TOPK = 16
PEER_DHALF = 128

VMEM_LIMIT = 56 * 1024 * 1024


def _cparams(sem):
    return pltpu.CompilerParams(dimension_semantics=sem, vmem_limit_bytes=VMEM_LIMIT)


def _dot(a, b, **kw):
    return jnp.dot(a, b, preferred_element_type=F32, **kw)


def _dot_nt(a, b):
    return lax.dot_general(a, b, (((1,), (1,)), ((), ())), preferred_element_type=F32)


def _dot_tn(a, b):
    return lax.dot_general(a, b, (((0,), (0,)), ((), ())), preferred_element_type=F32)


def _rms(x, g):
    return x * lax.rsqrt(jnp.mean(x * x, axis=-1, keepdims=True) + RMS_EPS) * g


def _silu(x):
    return x * jax.nn.sigmoid(x)


def _ada_kernel(c_ref, w_ref, b_ref, o_ref):
    o_ref[0] = _dot(_silu(c_ref[...]), w_ref[0], precision=lax.Precision.HIGHEST) + b_ref[0]


def _ada(c, ada_w, ada_b):
    depth, d, n = ada_w.shape
    bsz = c.shape[0]
    tn = 1024
    return pl.pallas_call(
        _ada_kernel,
        grid=(depth, n // tn),
        in_specs=[pl.BlockSpec((bsz, d), lambda l, j: (0, 0)),
                  pl.BlockSpec((1, d, tn), lambda l, j: (l, 0, j)),
                  pl.BlockSpec((1, 1, tn), lambda l, j: (l, 0, j))],
        out_specs=pl.BlockSpec((1, bsz, tn), lambda l, j: (l, 0, j)),
        out_shape=jax.ShapeDtypeStruct((depth, bsz, n), F32),
        compiler_params=_cparams(("arbitrary", "arbitrary")),
        name="ada_proj",
    )(c, ada_w, ada_b.reshape(depth, 1, n))


def _norm_mod_matmul_kernel(x_ref, g_ref, sc_ref, sh_ref, w_ref, o_ref, h_scr):
    @pl.when(pl.program_id(1) == 0)
    def _():
        h = _rms(x_ref[...], g_ref[...]) * (1.0 + sc_ref[0]) + sh_ref[0]
        h_scr[...] = h.astype(BF16)

    res = _dot(h_scr[...], w_ref[...]).astype(o_ref.dtype)
    if len(o_ref.shape) == 2:
        o_ref[...] = res
    else:
        for hh in range(o_ref.shape[0]):
            o_ref[hh] = res[:, hh * LANES:(hh + 1) * LANES]


def _norm_mod_matmul(x2, g, sc, sh, w, out_dtype, seq, head_major=False):
    t, d = x2.shape
    n = w.shape[1]
    tm = min(1024, seq)
    tn = 1024
    if head_major:
        out_spec = pl.BlockSpec((tn // LANES, tm, LANES), lambda i, j: (j, i, 0))
        out_shape = jax.ShapeDtypeStruct((n // LANES, t, LANES), out_dtype)
    else:
        out_spec = pl.BlockSpec((tm, tn), lambda i, j: (i, j))
        out_shape = jax.ShapeDtypeStruct((t, n), out_dtype)
    return pl.pallas_call(
        _norm_mod_matmul_kernel,
        grid=(t // tm, n // tn),
        in_specs=[pl.BlockSpec((tm, d), lambda i, j: (i, 0)),
                  pl.BlockSpec((1, d), lambda i, j: (0, 0)),
                  pl.BlockSpec((1, 1, d), lambda i, j: ((i * tm) // seq, 0, 0)),
                  pl.BlockSpec((1, 1, d), lambda i, j: ((i * tm) // seq, 0, 0)),
                  pl.BlockSpec((d, tn), lambda i, j: (0, j))],
        out_specs=out_spec,
        out_shape=out_shape,
        scratch_shapes=[pltpu.VMEM((tm, d), BF16)],
        compiler_params=_cparams(("arbitrary", "arbitrary")),
        name="norm_mod_matmul",
    )(x2, g.reshape(1, d), sc, sh, w)


LOG2E = 1.4426950408889634


def _attn_kernel(lam_ref, g_ref, slope_ref, q_ref, k_ref, v_ref, o_ref, s_scr, mx_scr, ls_scr, acc_scr,
                 *, lam_init, tq):
    qi = pl.program_id(2)
    slope = slope_ref[0] * LOG2E
    lane = lax.broadcasted_iota(jnp.int32, (1, 2 * ATT_DK), 1)
    q = (q_ref[...].astype(F32) * (ATT_DK ** -0.5 * LOG2E)).astype(BF16)
    zero = jnp.zeros_like(q)
    qq = jnp.concatenate([jnp.where(lane < ATT_DK, q, zero), jnp.where(lane >= ATT_DK, q, zero)], axis=0)
    col = lax.broadcasted_iota(jnp.int32, (1, tq), 1)

    def logits(j):
        k = k_ref[pl.ds(pl.multiple_of(j * tq, tq), tq), :]
        bias = slope * (col + (j - qi) * tq).astype(F32)
        return _dot_nt(qq, k) + bias

    def fold_max(s):
        m = s[:, :LANES]
        for c in range(1, tq // LANES):
            m = jnp.maximum(m, s[:, c * LANES:(c + 1) * LANES])
        mx_scr[...] = jnp.maximum(mx_scr[...], m)

    mx_scr[...] = jnp.full(mx_scr.shape, MASK_VALUE, F32)

    def pass_a(j, carry):
        s = logits(j)
        s_scr[j] = s
        fold_max(s)
        return carry

    lax.fori_loop(0, qi, pass_a, 0)
    row = lax.broadcasted_iota(jnp.int32, (2 * tq, tq), 0) & (tq - 1)
    keep = row >= lax.broadcasted_iota(jnp.int32, (2 * tq, tq), 1)
    s = jnp.where(keep, logits(qi), MASK_VALUE)
    s_scr[qi] = s
    fold_max(s)

    m_row = jnp.max(mx_scr[...], axis=1, keepdims=True)
    mx_scr[...] = jnp.broadcast_to(m_row, mx_scr.shape)
    ls_scr[...] = jnp.zeros(ls_scr.shape, F32)
    acc_scr[...] = jnp.zeros(acc_scr.shape, F32)

    def pass_b(j, carry):
        s = s_scr[j]
        m = mx_scr[...]
        ps = [jnp.exp2(s[:, c * LANES:(c + 1) * LANES] - m) for c in range(tq // LANES)]
        tot = ps[0]
        for p in ps[1:]:
            tot = tot + p
        ls_scr[...] += tot
        p = jnp.concatenate(ps, axis=1).astype(BF16)
        v = v_ref[pl.ds(pl.multiple_of(j * tq, tq), tq), :]
        acc_scr[...] += _dot(p, v)
        return carry

    lax.fori_loop(0, qi + 1, pass_b, 0)

    lq = lam_ref[...]
    lam = (jnp.exp(jnp.sum(lq[0:1] * lq[1:2], axis=1, keepdims=True))
           - jnp.exp(jnp.sum(lq[2:3] * lq[3:4], axis=1, keepdims=True)) + lam_init)
    o2 = acc_scr[...] / jnp.sum(ls_scr[...], axis=1, keepdims=True)
    o = o2[:tq] - lam * o2[tq:]
    o_ref[...] = _rms(o, g_ref[...]) * (1.0 - lam_init)


def _diff_attention(z_att, lam_qk, subln_g, layer, bsz, seq):
    t = z_att.shape[1]
    heads = z_att.shape[0] // 3
    tq = min(512, seq)
    nq = seq // tq
    lam_init = 0.8 - 0.6 * math.exp(-0.3 * layer)
    slopes = jnp.asarray([2.0 ** (-8.0 * (h + 1) / heads) for h in range(heads)], F32)
    slopes = jnp.broadcast_to(slopes[:, None, None], (heads, 1, tq))
    kern = functools.partial(_attn_kernel, lam_init=lam_init, tq=tq)
    return pl.pallas_call(
        kern,
        grid=(bsz, heads, nq),
        in_specs=[pl.BlockSpec((4, ATT_DK), lambda b, h, i: (0, 0)),
                  pl.BlockSpec((1, ATT_DV), lambda b, h, i: (0, 0)),
                  pl.BlockSpec((1, 1, tq), lambda b, h, i: (h, 0, 0)),
                  pl.BlockSpec((None, tq, ATT_DV), lambda b, h, i: (h, b * nq + i, 0)),
                  pl.BlockSpec((None, seq, ATT_DV), lambda b, h, i: (heads + h, b, 0)),
                  pl.BlockSpec((None, seq, ATT_DV), lambda b, h, i: (2 * heads + h, b, 0))],
        out_specs=pl.BlockSpec((None, tq, ATT_DV), lambda b, h, i: (h, b * nq + i, 0)),
        out_shape=jax.ShapeDtypeStruct((heads, t, ATT_DV), F32),
        scratch_shapes=[pltpu.VMEM((nq, 2 * tq, tq), F32), pltpu.VMEM((2 * tq, LANES), F32),
                        pltpu.VMEM((2 * tq, LANES), F32), pltpu.VMEM((2 * tq, ATT_DV), F32)],
        compiler_params=_cparams(("arbitrary", "arbitrary", "arbitrary")),
        name="diff_attention",
    )(lam_qk, subln_g.reshape(1, ATT_DV), slopes, z_att, z_att, z_att)


def _pool_kernel(p_ref, w_ref, sc_ref, o_ref, pad_scr, *, seq):
    g = pl.program_id(1)
    pad = max(POOL_WINDOWS)
    p = p_ref[...]
    pad_scr[0:pad, :] = jnp.zeros((pad, p.shape[1]), F32)
    pad_scr[pad:pad + seq, :] = p
    t1 = lax.broadcasted_iota(jnp.int32, (seq, 1), 0) + 1

    for gi, win in enumerate(POOL_WINDOWS):
        @pl.when(g == gi)
        def _(win=win):
            acc = p
            for s in range(1, win):
                acc = acc + pad_scr[pad - s:pad - s + seq, :]
            count = jnp.minimum(t1, win).astype(F32)
            pooled = acc / count - p
            y = _dot(pooled.astype(BF16), w_ref[0])
            o_ref[...] = y * sc_ref[0]


def _multiscale_pool(z_rest, pool_w, pool_scale, bsz, seq):
    t = z_rest.shape[0]
    groups, gw, _ = pool_w.shape
    kern = functools.partial(_pool_kernel, seq=seq)
    return pl.pallas_call(
        kern,
        grid=(bsz, groups),
        in_specs=[pl.BlockSpec((seq, gw), lambda b, g: (b, g)),
                  pl.BlockSpec((1, gw, gw), lambda b, g: (g, 0, 0)),
                  pl.BlockSpec((1, 1, gw), lambda b, g: (g, 0, 0))],
        out_specs=pl.BlockSpec((seq, gw), lambda b, g: (b, g)),
        out_shape=jax.ShapeDtypeStruct((t, groups * gw), F32),
        scratch_shapes=[pltpu.VMEM((seq + max(POOL_WINDOWS), gw), F32)],
        compiler_params=_cparams(("arbitrary", "arbitrary")),
        name="multiscale_pool",
    )(z_rest, pool_w.astype(BF16), pool_scale.reshape(groups, 1, gw))


def _log_sigmoid(z):
    return jnp.minimum(z, 0.0) - jnp.log1p(jnp.exp(-jnp.abs(z)))


def _logaddexp(a, b):
    return jnp.maximum(a, b) + jnp.log1p(jnp.exp(-jnp.abs(a - b)))


def _hgrn_kernel(lbp_ref, ng_ref, q_ref, f_ref, i_ref, g_ref, o_ref, st_scr, b_scr, k_scr, v_scr,
                 *, layer, seq, hb):
    chunk, sub = HGRN_CHUNK, HGRN_SUB
    lbp = lbp_ref[...]
    e = jnp.exp(lbp - jnp.max(lbp, axis=0, keepdims=True))
    prob = e / jnp.sum(e, axis=0, keepdims=True)
    cs = prob[0:1]
    for l in range(1, layer + 1):
        cs = cs + prob[l:l + 1]
    lb = cs - prob[0:1]
    log_lb = jnp.log(jnp.maximum(lb, LB_FLOOR))
    log_1m = jnp.log1p(-lb)

    tril = (lax.broadcasted_iota(jnp.int32, (chunk, chunk), 0)
            >= lax.broadcasted_iota(jnp.int32, (chunk, chunk), 1)).astype(F32)
    trow = lax.broadcasted_iota(jnp.int32, (sub, 1), 0)
    st_scr[...] = jnp.zeros(st_scr.shape, F32)
    ng = ng_ref[...]

    def head_chunk(hh, rows):
        hs = slice(hh * HGRN_D, (hh + 1) * HGRN_D)
        logf = _logaddexp(log_lb[:, hs], log_1m[:, hs] + _log_sigmoid(f_ref[rows, hs]))
        b = _dot(tril, logf, precision=lax.Precision.HIGHEST)
        kk = 1.0 - jnp.exp(logf)
        qs = _silu(q_ref[rows, hs])
        v = i_ref[rows, hs]
        b_scr[hh] = b
        k_scr[hh] = kk
        v_scr[hh] = v
        st = st_scr[hh]
        vb = v.astype(BF16)
        o_inter = _dot_nt((qs * jnp.exp(b)).astype(BF16), st.astype(BF16))

        outs = []
        for blk in range(chunk // sub):
            s0 = blk * sub
            b_i = b[s0:s0 + sub]
            q_i = qs[s0:s0 + sub]
            o_i = o_inter[s0:s0 + sub]
            if blk > 0:
                b_ref0 = b[s0:s0 + 1]
                k_prev = kk[:s0] * jnp.exp(b_ref0 - b[:s0])
                a = _dot_nt((q_i * jnp.exp(b_i - b_ref0)).astype(BF16), k_prev.astype(BF16))
                o_i = o_i + _dot(a.astype(BF16), vb[:s0])
            for s in range(sub):
                b_s = b_scr[hh, s0 + s:s0 + s + 1, :]
                k_s = k_scr[hh, s0 + s:s0 + s + 1, :]
                v_s = v_scr[hh, s0 + s:s0 + s + 1, :]
                dec = jnp.exp(jnp.where(trow >= s, b_i - b_s, MASK_VALUE))
                colv = jnp.sum(q_i * dec * k_s, axis=1, keepdims=True)
                o_i = o_i + colv * v_s
            outs.append(o_i)
        o = jnp.concatenate(outs, axis=0)
        o_ref[rows, hs] = _rms(o, ng) * _silu(g_ref[rows, hs])

        b_last = b[chunk - 1:chunk]
        k_dec = kk * jnp.exp(b_last - b)
        st_scr[hh] = jnp.exp(b_last) * st + _dot_tn(vb, k_dec.astype(BF16))

    def chunk_body(c, carry):
        rows = pl.ds(pl.multiple_of(c * chunk, chunk), chunk)
        for hh in range(hb):
            head_chunk(hh, rows)
        return carry

    lax.fori_loop(0, seq // chunk, chunk_body, 0)


def _hgrn2(z_rest, lower_bounds, norm_g, layer, bsz, seq, d_model):
    t = z_rest.shape[0]
    heads = d_model // HGRN_D
    depth = lower_bounds.shape[0]
    hb = 4
    hw = hb * HGRN_D
    ng = heads // hb
    kern = functools.partial(_hgrn_kernel, layer=layer, seq=seq, hb=hb)

    def col(k):
        return pl.BlockSpec((seq, hw), lambda b, h, k=k: (b, k * ng + h))

    return pl.pallas_call(
        kern,
        grid=(bsz, ng),
        in_specs=[pl.BlockSpec((depth, hw), lambda b, h: (0, h)),
                  pl.BlockSpec((1, HGRN_D), lambda b, h: (0, 0)),
                  col(1), col(2), col(3), col(4)],
        out_specs=pl.BlockSpec((seq, hw), lambda b, h: (b, h)),
        out_shape=jax.ShapeDtypeStruct((t, d_model), F32),
        scratch_shapes=[pltpu.VMEM((hb, HGRN_D, HGRN_D), F32),
                        pltpu.VMEM((hb, HGRN_CHUNK, HGRN_D), F32),
                        pltpu.VMEM((hb, HGRN_CHUNK, HGRN_D), F32),
                        pltpu.VMEM((hb, HGRN_CHUNK, HGRN_D), F32)],
        compiler_params=_cparams(("arbitrary", "arbitrary")),
        name="hgrn2",
    )(lower_bounds, norm_g.reshape(1, HGRN_D), z_rest, z_rest, z_rest, z_rest)


def _merge_kernel(a_ref, p_ref, r_ref, ga_ref, gp_ref, gr_ref, x_ref, g1_ref, w_ref, o_ref):
    a = jnp.concatenate([a_ref[hh] for hh in range(a_ref.shape[0])], axis=1)
    merged = (jax.nn.sigmoid(ga_ref[...]) * a + jax.nn.sigmoid(gp_ref[...]) * p_ref[...]
              + jax.nn.sigmoid(gr_ref[...]) * r_ref[...])
    o_ref[...] = x_ref[...] + g1_ref[0] * _dot(merged.astype(BF16), w_ref[...])


def _merge_out(a_out, p_out, r_out, z_rest, x2, g1, w_out, seq):
    t, d = x2.shape
    tm = min(256, seq)
    row = pl.BlockSpec((tm, d), lambda i: (i, 0))

    def gate(k):
        return pl.BlockSpec((tm, d), lambda i, k=k: (i, 5 + k))

    return pl.pallas_call(
        _merge_kernel,
        grid=(t // tm,),
        in_specs=[pl.BlockSpec((d // ATT_DV, tm, ATT_DV), lambda i: (0, i, 0)), row, row,
                  gate(0), gate(1), gate(2), row,
                  pl.BlockSpec((1, 1, d), lambda i: ((i * tm) // seq, 0, 0)),
                  pl.BlockSpec((d, d), lambda i: (0, 0))],
        out_specs=row,
        out_shape=jax.ShapeDtypeStruct((t, d), F32),
        compiler_params=_cparams(("arbitrary",)),
        name="merge_out",
    )(a_out, p_out, r_out, z_rest, z_rest, z_rest, x2, g1, w_out.astype(BF16))


def _peer_query_kernel(x_ref, g_ref, sc_ref, sh_ref, wq_ref, keys_ref, h_ref, s_ref):
    h = (_rms(x_ref[...], g_ref[...]) * (1.0 + sc_ref[0]) + sh_ref[0]).astype(BF16)
    h_ref[...] = h
    dq = 2 * PEER_DHALF
    for hd in range(PEER_HEADS):
        q = _dot(h, wq_ref[:, hd * dq:(hd + 1) * dq]).astype(BF16)
        for p in range(2):
            s_ref[hd, p] = _dot_nt(keys_ref[p], q[:, p * PEER_DHALF:(p + 1) * PEER_DHALF])


def _peer_query(x2, g, sc, sh, w_query, sub_keys, seq):
    t, d = x2.shape
    tm = min(512, seq)
    nq = w_query.shape[1]
    return pl.pallas_call(
        _peer_query_kernel,
        grid=(t // tm,),
        in_specs=[pl.BlockSpec((tm, d), lambda i: (i, 0)),
                  pl.BlockSpec((1, d), lambda i: (0, 0)),
                  pl.BlockSpec((1, 1, d), lambda i: ((i * tm) // seq, 0, 0)),
                  pl.BlockSpec((1, 1, d), lambda i: ((i * tm) // seq, 0, 0)),
                  pl.BlockSpec((d, nq), lambda i: (0, 0)),
                  pl.BlockSpec((2, PEER_NKEYS, PEER_DHALF), lambda i: (0, 0, 0))],
        out_specs=[pl.BlockSpec((tm, d), lambda i: (i, 0)),
                   pl.BlockSpec((PEER_HEADS, 2, PEER_NKEYS, tm), lambda i: (0, 0, 0, i))],
        out_shape=[jax.ShapeDtypeStruct((t, d), BF16),
                   jax.ShapeDtypeStruct((PEER_HEADS, 2, PEER_NKEYS, t), F32)],
        compiler_params=_cparams(("arbitrary",)),
        name="peer_query",
    )(x2, g.reshape(1, d), sc, sh, w_query.astype(BF16), sub_keys.astype(BF16))


NEG_BIG = -3.0e38


def _top_values(s, n):
    vals = []
    cur = s
    for k in range(n):
        m = jnp.max(cur, axis=0, keepdims=True)
        vals.append(m)
        if k + 1 < n:
            cur = jnp.where(cur == m, NEG_BIG, cur)
    return vals


_PEER_N = PEER_TOPK + 1
_PEER_PAIRS = [(i, j) for i in range(_PEER_N) for j in range(_PEER_N) if (i + 1) * (j + 1) <= _PEER_N]
_PEER_CAND_ROWS = -(-len(_PEER_PAIRS) // 8) * 8


def _peer_select_kernel(s_ref, c_ref, g_ref, eb_ref, cand_scr, wts_scr):
    n = _PEER_N
    tt = cand_scr.shape[1]
    npairs = len(_PEER_PAIRS)
    cand_scr[npairs:, :] = jnp.full((_PEER_CAND_ROWS - npairs, tt), NEG_BIG, F32)
    wts_scr[npairs:, :] = jnp.zeros((_PEER_CAND_ROWS - npairs, tt), F32)
    for hd in range(PEER_HEADS):
        sa = s_ref[hd, 0]
        sb = s_ref[hd, 1]
        ta = _top_values(sa, n)
        tb = _top_values(sb, n)
        ea_top = [jnp.exp(v - ta[0]) for v in ta]
        eb_top = [jnp.exp(v - tb[0]) for v in tb]
        for k, (i, j) in enumerate(_PEER_PAIRS):
            cand_scr[k:k + 1, :] = ta[i] + tb[j]
            wts_scr[k:k + 1, :] = ea_top[i] * eb_top[j]
        cand = cand_scr[...]
        top = _top_values(cand, n)
        tau = 0.5 * (top[PEER_TOPK - 1] + top[PEER_TOPK])
        zsum = jnp.sum(jnp.where(cand > tau, wts_scr[...], 0.0), axis=0, keepdims=True)
        c_ref[hd] = tau - sa
        g_ref[hd] = jnp.exp(sa - ta[0]) / zsum
        eb_ref[hd] = jnp.exp(sb - tb[0])


def _peer_select(scores):
    heads, _, nk, t = scores.shape
    tt = min(256, t)
    spec = pl.BlockSpec((heads, nk, tt), lambda i: (0, 0, i))
    shape = jax.ShapeDtypeStruct((heads, nk, t), F32)
    return pl.pallas_call(
        _peer_select_kernel,
        grid=(t // tt,),
        in_specs=[pl.BlockSpec((heads, 2, nk, tt), lambda i: (0, 0, 0, i))],
        out_specs=[spec, spec, spec],
        out_shape=[shape, shape, shape],
        scratch_shapes=[pltpu.VMEM((_PEER_CAND_ROWS, tt), F32), pltpu.VMEM((_PEER_CAND_ROWS, tt), F32)],
        compiler_params=_cparams(("arbitrary",)),
        name="peer_select",
    )(scores)


def _peer_dense_kernel(h_ref, u_ref, vt_ref, sb_ref, eb_ref, c_ref, g_ref, x_ref, g2_ref, fg_ref,
                       o_ref, acc_scr, act_scr, p_scr, *, final_norm, rows_per_step, tt):
    e = pl.program_id(1)

    @pl.when(e == 0)
    def _():
        acc_scr[...] = jnp.zeros(acc_scr.shape, F32)

    jg = 32

    def tile_body(ls, j0):
        w = [jnp.zeros((jg, LANES), F32) for _ in range(rows_per_step)]
        for hd in range(PEER_HEADS):
            sbv = sb_ref[hd, pl.ds(j0, jg), ls]
            ebv = eb_ref[hd, pl.ds(j0, jg), ls]
            for r in range(rows_per_step):
                w[r] = w[r] + jnp.where(sbv > c_ref[hd, r:r + 1, ls], ebv * g_ref[hd, r:r + 1, ls], 0.0)
        for r in range(rows_per_step):
            rows = pl.ds(pl.multiple_of(r * PEER_NKEYS + j0, jg), jg)
            a = act_scr[rows, ls]
            gelu = 0.5 * a * (1.0 + lax.erf(a * (2.0 ** -0.5)))
            p_scr[rows, ls] = (w[r] * gelu).astype(BF16)

    act_scr[...] = _dot_nt(u_ref[...], h_ref[...])
    for lc in range(tt // LANES):
        ls = slice(lc * LANES, (lc + 1) * LANES)

        def loop_body(jt, carry, ls=ls):
            tile_body(ls, pl.multiple_of(jt * jg, jg))
            return carry

        lax.fori_loop(0, PEER_NKEYS // jg, loop_body, 0)
    acc_scr[...] += _dot(vt_ref[...], p_scr[...])

    @pl.when(e == pl.num_programs(1) - 1)
    def _():
        y = x_ref[...] + g2_ref[0] * acc_scr[...].T
        if final_norm:
            y = _rms(y, fg_ref[...])
        o_ref[...] = y


def _peer_dense(h_bf, u_tab, v_tab, scores, c_arr, g_arr, eb_arr, x2, g2, final_g, final_norm, seq):
    t, d = x2.shape
    n_exp = u_tab.shape[0]
    heads, _, nk, _ = scores.shape
    tt = min(512, seq)
    rows_per_step = 8
    et = rows_per_step * nk
    vt_tiles = v_tab.astype(BF16).reshape(n_exp // et, et, d).transpose(0, 2, 1)
    kern = functools.partial(_peer_dense_kernel, final_norm=final_norm,
                             rows_per_step=rows_per_step, tt=tt)
    return pl.pallas_call(
        kern,
        grid=(t // tt, n_exp // et),
        in_specs=[pl.BlockSpec((tt, d), lambda i, e: (i, 0)),
                  pl.BlockSpec((et, d), lambda i, e: (e, 0)),
                  pl.BlockSpec((None, d, et), lambda i, e: (e, 0, 0)),
                  pl.BlockSpec((heads, None, nk, tt), lambda i, e: (0, 1, 0, i)),
                  pl.BlockSpec((heads, nk, tt), lambda i, e: (0, 0, i)),
                  pl.BlockSpec((heads, rows_per_step, tt), lambda i, e: (0, e, i)),
                  pl.BlockSpec((heads, rows_per_step, tt), lambda i, e: (0, e, i)),
                  pl.BlockSpec((tt, d), lambda i, e: (i, 0)),
                  pl.BlockSpec((1, 1, d), lambda i, e: ((i * tt) // seq, 0, 0)),
                  pl.BlockSpec((1, d), lambda i, e: (0, 0))],
        out_specs=pl.BlockSpec((tt, d), lambda i, e: (i, 0)),
        out_shape=jax.ShapeDtypeStruct((t, d), F32),
        scratch_shapes=[pltpu.VMEM((d, tt), F32), pltpu.VMEM((et, tt), F32),
                        pltpu.VMEM((et, tt), BF16)],
        compiler_params=_cparams(("arbitrary", "arbitrary")),
        name="peer_dense",
    )(h_bf, u_tab.astype(BF16), vt_tiles, scores, eb_arr, c_arr, g_arr, x2, g2,
      final_g.reshape(1, d))


def kernel(x, c, ada_w, ada_b, norm_mix_g, w_in, diff_lambda, diff_subln_g, pool_w, pool_scale,
           hgrn_lower_bounds, hgrn_norm_g, w_out, norm_ffn_g, peer_w_query, peer_sub_keys,
           peer_u, peer_v, final_g):
    bsz, seq, d = x.shape
    depth = ada_w.shape[0]
    att_cols = 3 * d
    x2 = x.reshape(bsz * seq, d)
    ada = _ada(c, ada_w, ada_b)
    for l in range(depth):
        sh1, sc1, g1, sh2, sc2, g2 = [ada[l, :, None, k * d:(k + 1) * d] for k in range(6)]
        w_l = w_in[l].astype(BF16)
        z_att = _norm_mod_matmul(x2, norm_mix_g[l], sc1, sh1, w_l[:, :att_cols], BF16, seq, head_major=True)
        z_rest = _norm_mod_matmul(x2, norm_mix_g[l], sc1, sh1, w_l[:, att_cols:], F32, seq)
        a_out = _diff_attention(z_att, diff_lambda[l], diff_subln_g[l], l, bsz, seq)
        p_out = _multiscale_pool(z_rest, pool_w[l], pool_scale[l], bsz, seq)
        r_out = _hgrn2(z_rest, hgrn_lower_bounds, hgrn_norm_g[l], l, bsz, seq, d)
        x2 = _merge_out(a_out, p_out, r_out, z_rest, x2, g1, w_out[l], seq)
        h_bf, scores = _peer_query(x2, norm_ffn_g[l], sc2, sh2, peer_w_query[l], peer_sub_keys[l], seq)
        c_arr, g_arr, eb_arr = _peer_select(scores)
        x2 = _peer_dense(h_bf, peer_u[l], peer_v[l], scores, c_arr, g_arr, eb_arr, x2, g2,
                         final_g, l == depth - 1, seq)
    return x2.reshape(bsz, seq, d)
```

```python
import functools
import math

import jax
import jax.numpy as jnp
from jax import lax
from jax.experimental import pallas as pl
from jax.experimental.pallas import tpu as pltpu

F32 = jnp.float32
BF16 = jnp.bfloat16

RMS_EPS = 1e-6
MASK_VALUE = -1e30
LB_FLOOR = 1e-20
LANES = 128

ATT_DK = 64
ATT_DV = 2 * ATT_DK
POOL_WINDOWS = (2, 4, 8, 16)
HGRN_D = 128
HGRN_CHUNK = 64
HGRN_SUB = 16
PEER_HEADS = 8
PEER_NKEYS = 128
PEER_TOPK = 16
PEER_DHALF = 128

VMEM_LIMIT = 56 * 1024 * 1024


def _cparams(sem):
    return pltpu.CompilerParams(dimension_semantics=sem, vmem_limit_bytes=VMEM_LIMIT)


def _dot(a, b, **kw):
    return jnp.dot(a, b, preferred_element_type=F32, **kw)


def _dot_nt(a, b):
    return lax.dot_general(a, b, (((1,), (1,)), ((), ())), preferred_element_type=F32)


def _dot_tn(a, b):
    return lax.dot_general(a, b, (((0,), (0,)), ((), ())), preferred_element_type=F32)


def _rms(x, g):
    return x * lax.rsqrt(jnp.mean(x * x, axis=-1, keepdims=True) + RMS_EPS) * g


def _silu(x):
    return x * jax.nn.sigmoid(x)


def _ada_kernel(c_ref, w_ref, b_ref, o_ref):
    o_ref[0] = _dot(_silu(c_ref[...]), w_ref[0], precision=lax.Precision.HIGHEST) + b_ref[0]


def _ada(c, ada_w, ada_b):
    depth, d, n = ada_w.shape
    bsz = c.shape[0]
    tn = 1024
    return pl.pallas_call(
        _ada_kernel,
        grid=(depth, n // tn),
        in_specs=[pl.BlockSpec((bsz, d), lambda l, j: (0, 0)),
                  pl.BlockSpec((1, d, tn), lambda l, j: (l, 0, j)),
                  pl.BlockSpec((1, 1, tn), lambda l, j: (l, 0, j))],
        out_specs=pl.BlockSpec((1, bsz, tn), lambda l, j: (l, 0, j)),
        out_shape=jax.ShapeDtypeStruct((depth, bsz, n), F32),
        compiler_params=_cparams(("arbitrary", "arbitrary")),
        name="ada_proj",
    )(c, ada_w, ada_b.reshape(depth, 1, n))


def _norm_mod_matmul_kernel(x_ref, g_ref, sc_ref, sh_ref, w_ref, o_ref, h_scr):
    @pl.when(pl.program_id(1) == 0)
    def _():
        h = _rms(x_ref[...], g_ref[...]) * (1.0 + sc_ref[0]) + sh_ref[0]
        h_scr[...] = h.astype(BF16)

    res = _dot(h_scr[...], w_ref[...]).astype(o_ref.dtype)
    if len(o_ref.shape) == 2:
        o_ref[...] = res
    else:
        for hh in range(o_ref.shape[0]):
            o_ref[hh] = res[:, hh * LANES:(hh + 1) * LANES]


def _norm_mod_matmul(x2, g, sc, sh, w, out_dtype, seq, head_major=False):
    t, d = x2.shape
    n = w.shape[1]
    tm = min(1024, seq)
    tn = 1024
    if head_major:
        out_spec = pl.BlockSpec((tn // LANES, tm, LANES), lambda i, j: (j, i, 0))
        out_shape = jax.ShapeDtypeStruct((n // LANES, t, LANES), out_dtype)
    else:
        out_spec = pl.BlockSpec((tm, tn), lambda i, j: (i, j))
        out_shape = jax.ShapeDtypeStruct((t, n), out_dtype)
    return pl.pallas_call(
        _norm_mod_matmul_kernel,
        grid=(t // tm, n // tn),
        in_specs=[pl.BlockSpec((tm, d), lambda i, j: (i, 0)),
                  pl.BlockSpec((1, d), lambda i, j: (0, 0)),
                  pl.BlockSpec((1, 1, d), lambda i, j: ((i * tm) // seq, 0, 0)),
                  pl.BlockSpec((1, 1, d), lambda i, j: ((i * tm) // seq, 0, 0)),
                  pl.BlockSpec((d, tn), lambda i, j: (0, j))],
        out_specs=out_spec,
        out_shape=out_shape,
        scratch_shapes=[pltpu.VMEM((tm, d), BF16)],
        compiler_params=_cparams(("arbitrary", "arbitrary")),
        name="norm_mod_matmul",
    )(x2, g.reshape(1, d), sc, sh, w)


LOG2E = 1.4426950408889634


def _attn_kernel(lam_ref, g_ref, slope_ref, q_ref, k_ref, v_ref, o_ref, s_scr, mx_scr, ls_scr, acc_scr,
                 *, lam_init, tq):
    qi = pl.program_id(2)
    slope = slope_ref[0] * LOG2E
    lane = lax.broadcasted_iota(jnp.int32, (1, 2 * ATT_DK), 1)
    q = (q_ref[...].astype(F32) * (ATT_DK ** -0.5 * LOG2E)).astype(BF16)
    zero = jnp.zeros_like(q)
    qq = jnp.concatenate([jnp.where(lane < ATT_DK, q, zero), jnp.where(lane >= ATT_DK, q, zero)], axis=0)
    col = lax.broadcasted_iota(jnp.int32, (1, tq), 1)

    def logits(j):
        k = k_ref[pl.ds(pl.multiple_of(j * tq, tq), tq), :]
        bias = slope * (col + (j - qi) * tq).astype(F32)
        return _dot_nt(qq, k) + bias

    def fold_max(s):
        m = s[:, :LANES]
        for c in range(1, tq // LANES):
            m = jnp.maximum(m, s[:, c * LANES:(c + 1) * LANES])
        mx_scr[...] = jnp.maximum(mx_scr[...], m)

    mx_scr[...] = jnp.full(mx_scr.shape, MASK_VALUE, F32)

    def pass_a(j, carry):
        s = logits(j)
        s_scr[j] = s
        fold_max(s)
        return carry

    lax.fori_loop(0, qi, pass_a, 0)
    row = lax.broadcasted_iota(jnp.int32, (2 * tq, tq), 0) & (tq - 1)
    keep = row >= lax.broadcasted_iota(jnp.int32, (2 * tq, tq), 1)
    s = jnp.where(keep, logits(qi), MASK_VALUE)
    s_scr[qi] = s
    fold_max(s)

    m_row = jnp.max(mx_scr[...], axis=1, keepdims=True)
    mx_scr[...] = jnp.broadcast_to(m_row, mx_scr.shape)
    ls_scr[...] = jnp.zeros(ls_scr.shape, F32)
    acc_scr[...] = jnp.zeros(acc_scr.shape, F32)

    def pass_b(j, carry):
        s = s_scr[j]
        m = mx_scr[...]
        ps = [jnp.exp2(s[:, c * LANES:(c + 1) * LANES] - m) for c in range(tq // LANES)]
        tot = ps[0]
        for p in ps[1:]:
            tot = tot + p
        ls_scr[...] += tot
        p = jnp.concatenate(ps, axis=1).astype(BF16)
        v = v_ref[pl.ds(pl.multiple_of(j * tq, tq), tq), :]
        acc_scr[...] += _dot(p, v)
        return carry

    lax.fori_loop(0, qi + 1, pass_b, 0)

    lq = lam_ref[...]
    lam = (jnp.exp(jnp.sum(lq[0:1] * lq[1:2], axis=1, keepdims=True))
           - jnp.exp(jnp.sum(lq[2:3] * lq[3:4], axis=1, keepdims=True)) + lam_init)
    o2 = acc_scr[...] / jnp.sum(ls_scr[...], axis=1, keepdims=True)
    o = o2[:tq] - lam * o2[tq:]
    o_ref[...] = _rms(o, g_ref[...]) * (1.0 - lam_init)


def _diff_attention(z_att, lam_qk, subln_g, layer, bsz, seq):
    t = z_att.shape[1]
    heads = z_att.shape[0] // 3
    tq = min(512, seq)
    nq = seq // tq
    lam_init = 0.8 - 0.6 * math.exp(-0.3 * layer)
    slopes = jnp.asarray([2.0 ** (-8.0 * (h + 1) / heads) for h in range(heads)], F32)
    slopes = jnp.broadcast_to(slopes[:, None, None], (heads, 1, tq))
    kern = functools.partial(_attn_kernel, lam_init=lam_init, tq=tq)
    return pl.pallas_call(
        kern,
        grid=(bsz, heads, nq),
        in_specs=[pl.BlockSpec((4, ATT_DK), lambda b, h, i: (0, 0)),
                  pl.BlockSpec((1, ATT_DV), lambda b, h, i: (0, 0)),
                  pl.BlockSpec((1, 1, tq), lambda b, h, i: (h, 0, 0)),
                  pl.BlockSpec((None, tq, ATT_DV), lambda b, h, i: (h, b * nq + i, 0)),
                  pl.BlockSpec((None, seq, ATT_DV), lambda b, h, i: (heads + h, b, 0)),
                  pl.BlockSpec((None, seq, ATT_DV), lambda b, h, i: (2 * heads + h, b, 0))],
        out_specs=pl.BlockSpec((None, tq, ATT_DV), lambda b, h, i: (h, b * nq + i, 0)),
        out_shape=jax.ShapeDtypeStruct((heads, t, ATT_DV), F32),
        scratch_shapes=[pltpu.VMEM((nq, 2 * tq, tq), F32), pltpu.VMEM((2 * tq, LANES), F32),
                        pltpu.VMEM((2 * tq, LANES), F32), pltpu.VMEM((2 * tq, ATT_DV), F32)],
        compiler_params=_cparams(("arbitrary", "arbitrary", "arbitrary")),
        name="diff_attention",
    )(lam_qk, subln_g.reshape(1, ATT_DV), slopes, z_att, z_att, z_att)


def _pool_kernel(p_ref, w_ref, sc_ref, o_ref, pad_scr, *, seq):
    g = pl.program_id(1)
    pad = max(POOL_WINDOWS)
    p = p_ref[...]
    pad_scr[0:pad, :] = jnp.zeros((pad, p.shape[1]), F32)
    pad_scr[pad:pad + seq, :] = p
    t1 = lax.broadcasted_iota(jnp.int32, (seq, 1), 0) + 1

    for gi, win in enumerate(POOL_WINDOWS):
        @pl.when(g == gi)
        def _(win=win):
            acc = p
            for s in range(1, win):
                acc = acc + pad_scr[pad - s:pad - s + seq, :]
            count = jnp.minimum(t1, win).astype(F32)
            pooled = acc / count - p
            y = _dot(pooled.astype(BF16), w_ref[0])
            o_ref[...] = y * sc_ref[0]


def _multiscale_pool(z_rest, pool_w, pool_scale, bsz, seq):
    t = z_rest.shape[0]
    groups, gw, _ = pool_w.shape
    kern = functools.partial(_pool_kernel, seq=seq)
    return pl.pallas_call(
        kern,
        grid=(bsz, groups),
        in_specs=[pl.BlockSpec((seq, gw), lambda b, g: (b, g)),
                  pl.BlockSpec((1, gw, gw), lambda b, g: (g, 0, 0)),
                  pl.BlockSpec((1, 1, gw), lambda b, g: (g, 0, 0))],
        out_specs=pl.BlockSpec((seq, gw), lambda b, g: (b, g)),
        out_shape=jax.ShapeDtypeStruct((t, groups * gw), F32),
        scratch_shapes=[pltpu.VMEM((seq + max(POOL_WINDOWS), gw), F32)],
        compiler_params=_cparams(("arbitrary", "arbitrary")),
        name="multiscale_pool",
    )(z_rest, pool_w.astype(BF16), pool_scale.reshape(groups, 1, gw))


def _log_sigmoid(z):
    return jnp.minimum(z, 0.0) - jnp.log1p(jnp.exp(-jnp.abs(z)))


def _logaddexp(a, b):
    return jnp.maximum(a, b) + jnp.log1p(jnp.exp(-jnp.abs(a - b)))


def _hgrn_kernel(lbp_ref, ng_ref, q_ref, f_ref, i_ref, g_ref, o_ref, st_scr, b_scr, k_scr, v_scr,
                 *, layer, seq, hb):
    chunk, sub = HGRN_CHUNK, HGRN_SUB
    lbp = lbp_ref[...]
    e = jnp.exp(lbp - jnp.max(lbp, axis=0, keepdims=True))
    prob = e / jnp.sum(e, axis=0, keepdims=True)
    cs = prob[0:1]
    for l in range(1, layer + 1):
        cs = cs + prob[l:l + 1]
    lb = cs - prob[0:1]
    log_lb = jnp.log(jnp.maximum(lb, LB_FLOOR))
    log_1m = jnp.log1p(-lb)

    tril = (lax.broadcasted_iota(jnp.int32, (chunk, chunk), 0)
            >= lax.broadcasted_iota(jnp.int32, (chunk, chunk), 1)).astype(F32)
    trow = lax.broadcasted_iota(jnp.int32, (sub, 1), 0)

    @pl.when(pl.program_id(2) == 0)
    def _():
        st_scr[...] = jnp.zeros(st_scr.shape, F32)

    ng = ng_ref[...]

    def head_chunk(hh, rows):
        hs = slice(hh * HGRN_D, (hh + 1) * HGRN_D)
        logf = _logaddexp(log_lb[:, hs], log_1m[:, hs] + _log_sigmoid(f_ref[rows, hs]))
        b = _dot(tril, logf, precision=lax.Precision.HIGHEST)
        kk = 1.0 - jnp.exp(logf)
        qs = _silu(q_ref[rows, hs])
        v = i_ref[rows, hs]
        b_scr[hh] = b
        k_scr[hh] = kk
        v_scr[hh] = v
        st = st_scr[hh]
        vb = v.astype(BF16)
        o_inter = _dot_nt((qs * jnp.exp(b)).astype(BF16), st.astype(BF16))

        outs = []
        for blk in range(chunk // sub):
            s0 = blk * sub
            b_i = b[s0:s0 + sub]
            q_i = qs[s0:s0 + sub]
            o_i = o_inter[s0:s0 + sub]
            if blk > 0:
                b_ref0 = b[s0:s0 + 1]
                k_prev = kk[:s0] * jnp.exp(b_ref0 - b[:s0])
                a = _dot_nt((q_i * jnp.exp(b_i - b_ref0)).astype(BF16), k_prev.astype(BF16))
                o_i = o_i + _dot(a.astype(BF16), vb[:s0])
            for s in range(sub):
                b_s = b_scr[hh, s0 + s:s0 + s + 1, :]
                k_s = k_scr[hh, s0 + s:s0 + s + 1, :]
                v_s = v_scr[hh, s0 + s:s0 + s + 1, :]
                dec = jnp.exp(jnp.where(trow >= s, b_i - b_s, MASK_VALUE))
                colv = jnp.sum(q_i * dec * k_s, axis=1, keepdims=True)
                o_i = o_i + colv * v_s
            outs.append(o_i)
        o = jnp.concatenate(outs, axis=0)
        o_ref[rows, hs] = _rms(o, ng) * _silu(g_ref[rows, hs])

        b_last = b[chunk - 1:chunk]
        k_dec = kk * jnp.exp(b_last - b)
        st_scr[hh] = jnp.exp(b_last) * st + _dot_tn(vb, k_dec.astype(BF16))

    def chunk_body(c, carry):
        rows = pl.ds(pl.multiple_of(c * chunk, chunk), chunk)
        for hh in range(hb):
            head_chunk(hh, rows)
        return carry

    lax.fori_loop(0, seq // chunk, chunk_body, 0)


def _hgrn2(z_rest, lower_bounds, norm_g, layer, bsz, seq, d_model):
    t = z_rest.shape[0]
    heads = d_model // HGRN_D
    depth = lower_bounds.shape[0]
    hb = 8
    hw = hb * HGRN_D
    ng = heads // hb
    ts = min(512, seq)
    ns = seq // ts
    kern = functools.partial(_hgrn_kernel, layer=layer, seq=ts, hb=hb)

    def col(k):
        return pl.BlockSpec((ts, hw), lambda b, h, s, k=k: (b * ns + s, k * ng + h))

    return pl.pallas_call(
        kern,
        grid=(bsz, ng, ns),
        in_specs=[pl.BlockSpec((depth, hw), lambda b, h, s: (0, h)),
                  pl.BlockSpec((1, HGRN_D), lambda b, h, s: (0, 0)),
                  col(1), col(2), col(3), col(4)],
        out_specs=pl.BlockSpec((ts, hw), lambda b, h, s: (b * ns + s, h)),
        out_shape=jax.ShapeDtypeStruct((t, d_model), F32),
        scratch_shapes=[pltpu.VMEM((hb, HGRN_D, HGRN_D), F32),
                        pltpu.VMEM((hb, HGRN_CHUNK, HGRN_D), F32),
                        pltpu.VMEM((hb, HGRN_CHUNK, HGRN_D), F32),
                        pltpu.VMEM((hb, HGRN_CHUNK, HGRN_D), F32)],
        compiler_params=_cparams(("arbitrary", "arbitrary", "arbitrary")),
        name="hgrn2",
    )(lower_bounds, norm_g.reshape(1, HGRN_D), z_rest, z_rest, z_rest, z_rest)


def _merge_kernel(a_ref, p_ref, r_ref, ga_ref, gp_ref, gr_ref, x_ref, g1_ref, w_ref, o_ref):
    a = jnp.concatenate([a_ref[hh] for hh in range(a_ref.shape[0])], axis=1)
    merged = (jax.nn.sigmoid(ga_ref[...]) * a + jax.nn.sigmoid(gp_ref[...]) * p_ref[...]
              + jax.nn.sigmoid(gr_ref[...]) * r_ref[...])
    o_ref[...] = x_ref[...] + g1_ref[0] * _dot(merged.astype(BF16), w_ref[...])


def _merge_out(a_out, p_out, r_out, z_rest, x2, g1, w_out, seq):
    t, d = x2.shape
    tm = min(256, seq)
    row = pl.BlockSpec((tm, d), lambda i: (i, 0))

    def gate(k):
        return pl.BlockSpec((tm, d), lambda i, k=k: (i, 5 + k))

    return pl.pallas_call(
        _merge_kernel,
        grid=(t // tm,),
        in_specs=[pl.BlockSpec((d // ATT_DV, tm, ATT_DV), lambda i: (0, i, 0)), row, row,
                  gate(0), gate(1), gate(2), row,
                  pl.BlockSpec((1, 1, d), lambda i: ((i * tm) // seq, 0, 0)),
                  pl.BlockSpec((d, d), lambda i: (0, 0))],
        out_specs=row,
        out_shape=jax.ShapeDtypeStruct((t, d), F32),
        compiler_params=_cparams(("arbitrary",)),
        name="merge_out",
    )(a_out, p_out, r_out, z_rest, z_rest, z_rest, x2, g1, w_out.astype(BF16))


def _peer_query_kernel(x_ref, g_ref, sc_ref, sh_ref, wq_ref, keys_ref, ht_ref, s_ref):
    hf = _rms(x_ref[...], g_ref[...]) * (1.0 + sc_ref[0]) + sh_ref[0]
    h = hf.astype(BF16)
    ht_ref[...] = hf.T.astype(BF16)
    dq = 2 * PEER_DHALF
    for hd in range(PEER_HEADS):
        q = _dot(h, wq_ref[:, hd * dq:(hd + 1) * dq]).astype(BF16)
        for p in range(2):
            s_ref[hd, p] = _dot_nt(keys_ref[p], q[:, p * PEER_DHALF:(p + 1) * PEER_DHALF])


def _peer_query(x2, g, sc, sh, w_query, sub_keys, seq):
    t, d = x2.shape
    tm = min(512, seq)
    nq = w_query.shape[1]
    return pl.pallas_call(
        _peer_query_kernel,
        grid=(t // tm,),
        in_specs=[pl.BlockSpec((tm, d), lambda i: (i, 0)),
                  pl.BlockSpec((1, d), lambda i: (0, 0)),
                  pl.BlockSpec((1, 1, d), lambda i: ((i * tm) // seq, 0, 0)),
                  pl.BlockSpec((1, 1, d), lambda i: ((i * tm) // seq, 0, 0)),
                  pl.BlockSpec((d, nq), lambda i: (0, 0)),
                  pl.BlockSpec((2, PEER_NKEYS, PEER_DHALF), lambda i: (0, 0, 0))],
        out_specs=[pl.BlockSpec((d, tm), lambda i: (0, i)),
                   pl.BlockSpec((PEER_HEADS, 2, PEER_NKEYS, tm), lambda i: (0, 0, 0, i))],
        out_shape=[jax.ShapeDtypeStruct((d, t), BF16),
                   jax.ShapeDtypeStruct((PEER_HEADS, 2, PEER_NKEYS, t), F32)],
        compiler_params=_cparams(("arbitrary",)),
        name="peer_query",
    )(x2, g.reshape(1, d), sc, sh, w_query.astype(BF16), sub_keys.astype(BF16))


NEG_BIG = -3.0e38


def _top_values(s, n):
    vals = []
    cur = s
    for k in range(n):
        m = jnp.max(cur, axis=0, keepdims=True)
        vals.append(m)
        if k + 1 < n:
            cur = jnp.where(cur == m, NEG_BIG, cur)
    return vals


_PEER_N = PEER_TOPK + 1
_PEER_PAIRS = [(i, j) for i in range(_PEER_N) for j in range(_PEER_N) if (i + 1) * (j + 1) <= _PEER_N]
_PEER_CAND_ROWS = -(-len(_PEER_PAIRS) // 8) * 8


def _peer_select_kernel(s_ref, c_ref, g_ref, eb_ref, cand_scr, wts_scr):
    n = _PEER_N
    tt = cand_scr.shape[1]
    npairs = len(_PEER_PAIRS)
    cand_scr[npairs:, :] = jnp.full((_PEER_CAND_ROWS - npairs, tt), NEG_BIG, F32)
    wts_scr[npairs:, :] = jnp.zeros((_PEER_CAND_ROWS - npairs, tt), F32)
    for hd in range(PEER_HEADS):
        sa = s_ref[hd, 0]
        sb = s_ref[hd, 1]
        ta = _top_values(sa, n)
        tb = _top_values(sb, n)
        ea_top = [jnp.exp(v - ta[0]) for v in ta]
        eb_top = [jnp.exp(v - tb[0]) for v in tb]
        for k, (i, j) in enumerate(_PEER_PAIRS):
            cand_scr[k:k + 1, :] = ta[i] + tb[j]
            wts_scr[k:k + 1, :] = ea_top[i] * eb_top[j]
        cand = cand_scr[...]
        top = _top_values(cand, n)
        tau = 0.5 * (top[PEER_TOPK - 1] + top[PEER_TOPK])
        zsum = jnp.sum(jnp.where(cand > tau, wts_scr[...], 0.0), axis=0, keepdims=True)
        c_ref[hd] = tau - sa
        g_ref[hd] = jnp.exp(sa - ta[0]) / zsum
        eb_ref[hd] = jnp.exp(sb - tb[0])


def _peer_select(scores):
    heads, _, nk, t = scores.shape
    tt = min(256, t)
    spec = pl.BlockSpec((heads, nk, tt), lambda i: (0, 0, i))
    shape = jax.ShapeDtypeStruct((heads, nk, t), F32)
    return pl.pallas_call(
        _peer_select_kernel,
        grid=(t // tt,),
        in_specs=[pl.BlockSpec((heads, 2, nk, tt), lambda i: (0, 0, 0, i))],
        out_specs=[spec, spec, spec],
        out_shape=[shape, shape, shape],
        scratch_shapes=[pltpu.VMEM((_PEER_CAND_ROWS, tt), F32), pltpu.VMEM((_PEER_CAND_ROWS, tt), F32)],
        compiler_params=_cparams(("arbitrary",)),
        name="peer_select",
    )(scores)


def _peer_dense_kernel(ht_ref, u_ref, vt_ref, sb_ref, eb_ref, c_ref, g_ref, x_ref, g2_ref, fg_ref,
                       o_ref, acc_scr, act_scr, p_scr, *, final_norm, rows_per_step, tt):
    e = pl.program_id(1)

    @pl.when(e == 0)
    def _():
        acc_scr[...] = jnp.zeros(acc_scr.shape, F32)

    jg = 32

    def tile_body(ls, j0):
        w = [jnp.zeros((jg, LANES), F32) for _ in range(rows_per_step)]
        for hd in range(PEER_HEADS):
            sbv = sb_ref[hd, pl.ds(j0, jg), ls]
            ebv = eb_ref[hd, pl.ds(j0, jg), ls]
            for r in range(rows_per_step):
                w[r] = w[r] + jnp.where(sbv > c_ref[hd, r:r + 1, ls], ebv * g_ref[hd, r:r + 1, ls], 0.0)
        for r in range(rows_per_step):
            rows = pl.ds(pl.multiple_of(r * PEER_NKEYS + j0, jg), jg)
            a = act_scr[rows, ls]
            gelu = 0.5 * a * (1.0 + lax.erf(a * (2.0 ** -0.5)))
            p_scr[rows, ls] = (w[r] * gelu).astype(BF16)

    act_scr[...] = _dot(u_ref[...], ht_ref[...])
    for lc in range(tt // LANES):
        ls = slice(lc * LANES, (lc + 1) * LANES)

        def loop_body(jt, carry, ls=ls):
            tile_body(ls, pl.multiple_of(jt * jg, jg))
            return carry

        lax.fori_loop(0, PEER_NKEYS // jg, loop_body, 0)
    acc_scr[...] += _dot(vt_ref[...], p_scr[...])

    @pl.when(e == pl.num_programs(1) - 1)
    def _():
        y = x_ref[...] + g2_ref[0] * acc_scr[...].T
        if final_norm:
            y = _rms(y, fg_ref[...])
        o_ref[...] = y


def _peer_dense(ht_bf, u_tab, v_tab, scores, c_arr, g_arr, eb_arr, x2, g2, final_g, final_norm, seq):
    t, d = x2.shape
    n_exp = u_tab.shape[0]
    heads, _, nk, _ = scores.shape
    tt = min(512, seq)
    rows_per_step = 8
    et = rows_per_step * nk
    vt_tiles = v_tab.astype(BF16).reshape(n_exp // et, et, d).transpose(0, 2, 1)
    kern = functools.partial(_peer_dense_kernel, final_norm=final_norm,
                             rows_per_step=rows_per_step, tt=tt)
    return pl.pallas_call(
        kern,
        grid=(t // tt, n_exp // et),
        in_specs=[pl.BlockSpec((d, tt), lambda i, e: (0, i)),
                  pl.BlockSpec((et, d), lambda i, e: (e, 0)),
                  pl.BlockSpec((None, d, et), lambda i, e: (e, 0, 0)),
                  pl.BlockSpec((heads, None, nk, tt), lambda i, e: (0, 1, 0, i)),
                  pl.BlockSpec((heads, nk, tt), lambda i, e: (0, 0, i)),
                  pl.BlockSpec((heads, rows_per_step, tt), lambda i, e: (0, e, i)),
                  pl.BlockSpec((heads, rows_per_step, tt), lambda i, e: (0, e, i)),
                  pl.BlockSpec((tt, d), lambda i, e: (i, 0)),
                  pl.BlockSpec((1, 1, d), lambda i, e: ((i * tt) // seq, 0, 0)),
                  pl.BlockSpec((1, d), lambda i, e: (0, 0))],
        out_specs=pl.BlockSpec((tt, d), lambda i, e: (i, 0)),
        out_shape=jax.ShapeDtypeStruct((t, d), F32),
        scratch_shapes=[pltpu.VMEM((d, tt), F32), pltpu.VMEM((et, tt), F32),
                        pltpu.VMEM((et, tt), BF16)],
        compiler_params=_cparams(("arbitrary", "arbitrary")),
        name="peer_dense",
    )(ht_bf, u_tab.astype(BF16), vt_tiles, scores, eb_arr, c_arr, g_arr, x2, g2,
      final_g.reshape(1, d))


def kernel(x, c, ada_w, ada_b, norm_mix_g, w_in, diff_lambda, diff_subln_g, pool_w, pool_scale,
           hgrn_lower_bounds, hgrn_norm_g, w_out, norm_ffn_g, peer_w_query, peer_sub_keys,
           peer_u, peer_v, final_g):
    bsz, seq, d = x.shape
    depth = ada_w.shape[0]
    att_cols = 3 * d
    x2 = x.reshape(bsz * seq, d)
    ada = _ada(c, ada_w, ada_b)
    for l in range(depth):
        sh1, sc1, g1, sh2, sc2, g2 = [ada[l, :, None, k * d:(k + 1) * d] for k in range(6)]
        w_l = w_in[l].astype(BF16)
        z_att = _norm_mod_matmul(x2, norm_mix_g[l], sc1, sh1, w_l[:, :att_cols], BF16, seq, head_major=True)
        z_rest = _norm_mod_matmul(x2, norm_mix_g[l], sc1, sh1, w_l[:, att_cols:], F32, seq)
        a_out = _diff_attention(z_att, diff_lambda[l], diff_subln_g[l], l, bsz, seq)
        p_out = _multiscale_pool(z_rest, pool_w[l], pool_scale[l], bsz, seq)
        r_out = _hgrn2(z_rest, hgrn_lower_bounds, hgrn_norm_g[l], l, bsz, seq, d)
        x2 = _merge_out(a_out, p_out, r_out, z_rest, x2, g1, w_out[l], seq)
        ht_bf, scores = _peer_query(x2, norm_ffn_g[l], sc2, sh2, peer_w_query[l], peer_sub_keys[l], seq)
        c_arr, g_arr, eb_arr = _peer_select(scores)
        x2 = _peer_dense(ht_bf, peer_u[l], peer_v[l], scores, c_arr, g_arr, eb_arr, x2, g2,
                         final_g, l == depth - 1, seq)
    return x2.reshape(bsz, seq, d)
```

```python
import functools
import math

import jax
import jax.numpy as jnp
from jax import lax
from jax.experimental import pallas as pl
from jax.experimental.pallas import tpu as pltpu

F32 = jnp.float32
BF16 = jnp.bfloat16

RMS_EPS = 1e-6
MASK_VALUE = -1e30
LB_FLOOR = 1e-20
LANES = 128

ATT_DK = 64
ATT_DV = 2 * ATT_DK
POOL_WINDOWS = (2, 4, 8, 16)
HGRN_D = 128
HGRN_CHUNK = 64
HGRN_SUB = 16
PEER_HEADS = 8
PEER_NKEYS = 128
PEER_TOPK = 16
PEER_DHALF = 128

VMEM_LIMIT = 56 * 1024 * 1024


def _cparams(sem):
    return pltpu.CompilerParams(dimension_semantics=sem, vmem_limit_bytes=VMEM_LIMIT)


def _dot(a, b, **kw):
    return jnp.dot(a, b, preferred_element_type=F32, **kw)


def _dot_nt(a, b):
    return lax.dot_general(a, b, (((1,), (1,)), ((), ())), preferred_element_type=F32)


def _dot_tn(a, b):
    return lax.dot_general(a, b, (((0,), (0,)), ((), ())), preferred_element_type=F32)


def _rms(x, g):
    return x * lax.rsqrt(jnp.mean(x * x, axis=-1, keepdims=True) + RMS_EPS) * g


def _silu(x):
    return x * jax.nn.sigmoid(x)


def _ada_kernel(c_ref, w_ref, b_ref, o_ref):
    o_ref[0] = _dot(_silu(c_ref[...]), w_ref[0], precision=lax.Precision.HIGHEST) + b_ref[0]


def _ada(c, ada_w, ada_b):
    depth, d, n = ada_w.shape
    bsz = c.shape[0]
    tn = 1024
    return pl.pallas_call(
        _ada_kernel,
        grid=(depth, n // tn),
        in_specs=[pl.BlockSpec((bsz, d), lambda l, j: (0, 0)),
                  pl.BlockSpec((1, d, tn), lambda l, j: (l, 0, j)),
                  pl.BlockSpec((1, 1, tn), lambda l, j: (l, 0, j))],
        out_specs=pl.BlockSpec((1, bsz, tn), lambda l, j: (l, 0, j)),
        out_shape=jax.ShapeDtypeStruct((depth, bsz, n), F32),
        compiler_params=_cparams(("arbitrary", "arbitrary")),
        name="ada_proj",
    )(c, ada_w, ada_b.reshape(depth, 1, n))


def _norm_mod_matmul_kernel(x_ref, g_ref, sc_ref, sh_ref, w_ref, o_ref, h_scr):
    @pl.when(pl.program_id(1) == 0)
    def _():
        h = _rms(x_ref[...], g_ref[...]) * (1.0 + sc_ref[0]) + sh_ref[0]
        h_scr[...] = h.astype(BF16)

    res = _dot(h_scr[...], w_ref[...]).astype(o_ref.dtype)
    if len(o_ref.shape) == 2:
        o_ref[...] = res
    else:
        for hh in range(o_ref.shape[0]):
            o_ref[hh] = res[:, hh * LANES:(hh + 1) * LANES]


def _norm_mod_matmul(x2, g, sc, sh, w, out_dtype, seq, head_major=False):
    t, d = x2.shape
    n = w.shape[1]
    tm = min(1024, seq)
    tn = 1024
    if head_major:
        out_spec = pl.BlockSpec((tn // LANES, tm, LANES), lambda i, j: (j, i, 0))
        out_shape = jax.ShapeDtypeStruct((n // LANES, t, LANES), out_dtype)
    else:
        out_spec = pl.BlockSpec((tm, tn), lambda i, j: (i, j))
        out_shape = jax.ShapeDtypeStruct((t, n), out_dtype)
    return pl.pallas_call(
        _norm_mod_matmul_kernel,
        grid=(t // tm, n // tn),
        in_specs=[pl.BlockSpec((tm, d), lambda i, j: (i, 0)),
                  pl.BlockSpec((1, d), lambda i, j: (0, 0)),
                  pl.BlockSpec((1, 1, d), lambda i, j: ((i * tm) // seq, 0, 0)),
                  pl.BlockSpec((1, 1, d), lambda i, j: ((i * tm) // seq, 0, 0)),
                  pl.BlockSpec((d, tn), lambda i, j: (0, j))],
        out_specs=out_spec,
        out_shape=out_shape,
        scratch_shapes=[pltpu.VMEM((tm, d), BF16)],
        compiler_params=_cparams(("arbitrary", "arbitrary")),
        name="norm_mod_matmul",
    )(x2, g.reshape(1, d), sc, sh, w)


LOG2E = 1.4426950408889634


def _attn_kernel(lam_ref, g_ref, slope_ref, q_ref, k_ref, v_ref, o_ref, s_scr, mx_scr, ls_scr, acc_scr,
                 *, lam_init, tq):
    qi = pl.program_id(2)
    slope = slope_ref[0] * LOG2E
    lane = lax.broadcasted_iota(jnp.int32, (1, 2 * ATT_DK), 1)
    q = (q_ref[...].astype(F32) * (ATT_DK ** -0.5 * LOG2E)).astype(BF16)
    zero = jnp.zeros_like(q)
    qq = jnp.concatenate([jnp.where(lane < ATT_DK, q, zero), jnp.where(lane >= ATT_DK, q, zero)], axis=0)
    col = lax.broadcasted_iota(jnp.int32, (1, tq), 1)

    def logits(j):
        k = k_ref[pl.ds(pl.multiple_of(j * tq, tq), tq), :]
        bias = slope * (col + (j - qi) * tq).astype(F32)
        return _dot_nt(qq, k) + bias

    def fold_max(s):
        m = s[:, :LANES]
        for c in range(1, tq // LANES):
            m = jnp.maximum(m, s[:, c * LANES:(c + 1) * LANES])
        mx_scr[...] = jnp.maximum(mx_scr[...], m)

    mx_scr[...] = jnp.full(mx_scr.shape, MASK_VALUE, F32)

    def pass_a(j, carry):
        s = logits(j)
        s_scr[j] = s
        fold_max(s)
        return carry

    lax.fori_loop(0, qi, pass_a, 0)
    row = lax.broadcasted_iota(jnp.int32, (2 * tq, tq), 0) & (tq - 1)
    keep = row >= lax.broadcasted_iota(jnp.int32, (2 * tq, tq), 1)
    s = jnp.where(keep, logits(qi), MASK_VALUE)
    s_scr[qi] = s
    fold_max(s)

    m_row = jnp.max(mx_scr[...], axis=1, keepdims=True)
    mx_scr[...] = jnp.broadcast_to(m_row, mx_scr.shape)
    ls_scr[...] = jnp.zeros(ls_scr.shape, F32)
    acc_scr[...] = jnp.zeros(acc_scr.shape, F32)

    def pass_b(j, carry):
        s = s_scr[j]
        m = mx_scr[...]
        ps = [jnp.exp2(s[:, c * LANES:(c + 1) * LANES] - m) for c in range(tq // LANES)]
        tot = ps[0]
        for p in ps[1:]:
            tot = tot + p
        ls_scr[...] += tot
        p = jnp.concatenate(ps, axis=1).astype(BF16)
        v = v_ref[pl.ds(pl.multiple_of(j * tq, tq), tq), :]
        acc_scr[...] += _dot(p, v)
        return carry

    lax.fori_loop(0, qi + 1, pass_b, 0)

    lq = lam_ref[...]
    lam = (jnp.exp(jnp.sum(lq[0:1] * lq[1:2], axis=1, keepdims=True))
           - jnp.exp(jnp.sum(lq[2:3] * lq[3:4], axis=1, keepdims=True)) + lam_init)
    o2 = acc_scr[...] / jnp.sum(ls_scr[...], axis=1, keepdims=True)
    o = o2[:tq] - lam * o2[tq:]
    o_ref[...] = (_rms(o, g_ref[...]) * (1.0 - lam_init)).astype(o_ref.dtype)


def _diff_attention(z_att, lam_qk, subln_g, layer, bsz, seq):
    t = z_att.shape[1]
    heads = z_att.shape[0] // 3
    tq = min(512, seq)
    nq = seq // tq
    lam_init = 0.8 - 0.6 * math.exp(-0.3 * layer)
    slopes = jnp.asarray([2.0 ** (-8.0 * (h + 1) / heads) for h in range(heads)], F32)
    slopes = jnp.broadcast_to(slopes[:, None, None], (heads, 1, tq))
    kern = functools.partial(_attn_kernel, lam_init=lam_init, tq=tq)
    return pl.pallas_call(
        kern,
        grid=(bsz, heads, nq),
        in_specs=[pl.BlockSpec((4, ATT_DK), lambda b, h, i: (0, 0)),
                  pl.BlockSpec((1, ATT_DV), lambda b, h, i: (0, 0)),
                  pl.BlockSpec((1, 1, tq), lambda b, h, i: (h, 0, 0)),
                  pl.BlockSpec((None, tq, ATT_DV), lambda b, h, i: (h, b * nq + i, 0)),
                  pl.BlockSpec((None, seq, ATT_DV), lambda b, h, i: (heads + h, b, 0)),
                  pl.BlockSpec((None, seq, ATT_DV), lambda b, h, i: (2 * heads + h, b, 0))],
        out_specs=pl.BlockSpec((None, tq, ATT_DV), lambda b, h, i: (h, b * nq + i, 0)),
        out_shape=jax.ShapeDtypeStruct((heads, t, ATT_DV), BF16),
        scratch_shapes=[pltpu.VMEM((nq, 2 * tq, tq), F32), pltpu.VMEM((2 * tq, LANES), F32),
                        pltpu.VMEM((2 * tq, LANES), F32), pltpu.VMEM((2 * tq, ATT_DV), F32)],
        compiler_params=_cparams(("arbitrary", "arbitrary", "arbitrary")),
        name="diff_attention",
    )(lam_qk, subln_g.reshape(1, ATT_DV), slopes, z_att, z_att, z_att)


def _pool_kernel(p_ref, w_ref, sc_ref, o_ref, pad_scr, *, seq):
    g = pl.program_id(1)
    pad = max(POOL_WINDOWS)
    p = p_ref[...].astype(F32)
    pad_scr[0:pad, :] = jnp.zeros((pad, p.shape[1]), F32)
    pad_scr[pad:pad + seq, :] = p
    t1 = lax.broadcasted_iota(jnp.int32, (seq, 1), 0) + 1

    for gi, win in enumerate(POOL_WINDOWS):
        @pl.when(g == gi)
        def _(win=win):
            acc = p
            for s in range(1, win):
                acc = acc + pad_scr[pad - s:pad - s + seq, :]
            count = jnp.minimum(t1, win).astype(F32)
            pooled = acc / count - p
            y = _dot(pooled.astype(BF16), w_ref[0])
            o_ref[...] = (y * sc_ref[0]).astype(o_ref.dtype)


def _multiscale_pool(z_rest, pool_w, pool_scale, bsz, seq):
    t = z_rest.shape[0]
    groups, gw, _ = pool_w.shape
    kern = functools.partial(_pool_kernel, seq=seq)
    return pl.pallas_call(
        kern,
        grid=(bsz, groups),
        in_specs=[pl.BlockSpec((seq, gw), lambda b, g: (b, g)),
                  pl.BlockSpec((1, gw, gw), lambda b, g: (g, 0, 0)),
                  pl.BlockSpec((1, 1, gw), lambda b, g: (g, 0, 0))],
        out_specs=pl.BlockSpec((seq, gw), lambda b, g: (b, g)),
        out_shape=jax.ShapeDtypeStruct((t, groups * gw), BF16),
        scratch_shapes=[pltpu.VMEM((seq + max(POOL_WINDOWS), gw), F32)],
        compiler_params=_cparams(("arbitrary", "arbitrary")),
        name="multiscale_pool",
    )(z_rest, pool_w.astype(BF16), pool_scale.reshape(groups, 1, gw))


def _log_sigmoid(z):
    return jnp.minimum(z, 0.0) - jnp.log1p(jnp.exp(-jnp.abs(z)))


def _logaddexp(a, b):
    return jnp.maximum(a, b) + jnp.log1p(jnp.exp(-jnp.abs(a - b)))


def _hgrn_kernel(lbp_ref, ng_ref, q_ref, f_ref, i_ref, g_ref, o_ref, st_scr, b_scr, k_scr, v_scr,
                 *, layer, seq, hb):
    chunk, sub = HGRN_CHUNK, HGRN_SUB
    lbp = lbp_ref[...]
    e = jnp.exp(lbp - jnp.max(lbp, axis=0, keepdims=True))
    prob = e / jnp.sum(e, axis=0, keepdims=True)
    cs = prob[0:1]
    for l in range(1, layer + 1):
        cs = cs + prob[l:l + 1]
    lb = cs - prob[0:1]
    log_lb = jnp.log(jnp.maximum(lb, LB_FLOOR))
    log_1m = jnp.log1p(-lb)

    tril = (lax.broadcasted_iota(jnp.int32, (chunk, chunk), 0)
            >= lax.broadcasted_iota(jnp.int32, (chunk, chunk), 1)).astype(F32)
    trow = lax.broadcasted_iota(jnp.int32, (sub, 1), 0)

    @pl.when(pl.program_id(2) == 0)
    def _():
        st_scr[...] = jnp.zeros(st_scr.shape, F32)

    ng = ng_ref[...]

    def head_chunk(hh, rows):
        hs = slice(hh * HGRN_D, (hh + 1) * HGRN_D)
        logf = _logaddexp(log_lb[:, hs], log_1m[:, hs] + _log_sigmoid(f_ref[rows, hs]))
        b = _dot(tril, logf, precision=lax.Precision.HIGHEST)
        kk = 1.0 - jnp.exp(logf)
        qs = _silu(q_ref[rows, hs].astype(F32))
        v = i_ref[rows, hs].astype(F32)
        b_scr[hh] = b
        k_scr[hh] = kk
        v_scr[hh] = v
        st = st_scr[hh]
        vb = v.astype(BF16)
        o_inter = _dot_nt((qs * jnp.exp(b)).astype(BF16), st.astype(BF16))

        outs = []
        for blk in range(chunk // sub):
            s0 = blk * sub
            b_i = b[s0:s0 + sub]
            q_i = qs[s0:s0 + sub]
            o_i = o_inter[s0:s0 + sub]
            if blk > 0:
                b_ref0 = b[s0:s0 + 1]
                k_prev = kk[:s0] * jnp.exp(b_ref0 - b[:s0])
                a = _dot_nt((q_i * jnp.exp(b_i - b_ref0)).astype(BF16), k_prev.astype(BF16))
                o_i = o_i + _dot(a.astype(BF16), vb[:s0])
            for s in range(sub):
                b_s = b_scr[hh, s0 + s:s0 + s + 1, :]
                k_s = k_scr[hh, s0 + s:s0 + s + 1, :]
                v_s = v_scr[hh, s0 + s:s0 + s + 1, :]
                dec = jnp.exp(jnp.where(trow >= s, b_i - b_s, MASK_VALUE))
                colv = jnp.sum(q_i * dec * k_s, axis=1, keepdims=True)
                o_i = o_i + colv * v_s
            outs.append(o_i)
        o = jnp.concatenate(outs, axis=0)
        o_ref[rows, hs] = (_rms(o, ng) * _silu(g_ref[rows, hs].astype(F32))).astype(o_ref.dtype)

        b_last = b[chunk - 1:chunk]
        k_dec = kk * jnp.exp(b_last - b)
        st_scr[hh] = jnp.exp(b_last) * st + _dot_tn(vb, k_dec.astype(BF16))

    def chunk_body(c, carry):
        rows = pl.ds(pl.multiple_of(c * chunk, chunk), chunk)
        for hh in range(hb):
            head_chunk(hh, rows)
        return carry

    lax.fori_loop(0, seq // chunk, chunk_body, 0)


def _hgrn2(z_rest, z_f, lower_bounds, norm_g, layer, bsz, seq, d_model):
    t = z_rest.shape[0]
    heads = d_model // HGRN_D
    depth = lower_bounds.shape[0]
    hb = 8
    hw = hb * HGRN_D
    ng = heads // hb
    ts = min(512, seq)
    ns = seq // ts
    kern = functools.partial(_hgrn_kernel, layer=layer, seq=ts, hb=hb)

    def col(k):
        return pl.BlockSpec((ts, hw), lambda b, h, s, k=k: (b * ns + s, k * ng + h))

    return pl.pallas_call(
        kern,
        grid=(bsz, ng, ns),
        in_specs=[pl.BlockSpec((depth, hw), lambda b, h, s: (0, h)),
                  pl.BlockSpec((1, HGRN_D), lambda b, h, s: (0, 0)),
                  col(1), pl.BlockSpec((ts, hw), lambda b, h, s: (b * ns + s, h)), col(3), col(4)],
        out_specs=pl.BlockSpec((ts, hw), lambda b, h, s: (b * ns + s, h)),
        out_shape=jax.ShapeDtypeStruct((t, d_model), BF16),
        scratch_shapes=[pltpu.VMEM((hb, HGRN_D, HGRN_D), F32),
                        pltpu.VMEM((hb, HGRN_CHUNK, HGRN_D), F32),
                        pltpu.VMEM((hb, HGRN_CHUNK, HGRN_D), F32),
                        pltpu.VMEM((hb, HGRN_CHUNK, HGRN_D), F32)],
        compiler_params=_cparams(("arbitrary", "arbitrary", "arbitrary")),
        name="hgrn2",
    )(lower_bounds, norm_g.reshape(1, HGRN_D), z_rest, z_f, z_rest, z_rest)


def _merge_kernel(a_ref, p_ref, r_ref, ga_ref, gp_ref, gr_ref, x_ref, g1_ref, w_ref, o_ref):
    a = jnp.concatenate([a_ref[hh] for hh in range(a_ref.shape[0])], axis=1)
    merged = (jax.nn.sigmoid(ga_ref[...].astype(F32)) * a + jax.nn.sigmoid(gp_ref[...].astype(F32)) * p_ref[...]
              + jax.nn.sigmoid(gr_ref[...].astype(F32)) * r_ref[...])
    o_ref[...] = x_ref[...] + g1_ref[0] * _dot(merged.astype(BF16), w_ref[...])


def _merge_out(a_out, p_out, r_out, z_rest, x2, g1, w_out, seq):
    t, d = x2.shape
    tm = min(256, seq)
    row = pl.BlockSpec((tm, d), lambda i: (i, 0))

    def gate(k):
        return pl.BlockSpec((tm, d), lambda i, k=k: (i, 5 + k))

    return pl.pallas_call(
        _merge_kernel,
        grid=(t // tm,),
        in_specs=[pl.BlockSpec((d // ATT_DV, tm, ATT_DV), lambda i: (0, i, 0)), row, row,
                  gate(0), gate(1), gate(2), row,
                  pl.BlockSpec((1, 1, d), lambda i: ((i * tm) // seq, 0, 0)),
                  pl.BlockSpec((d, d), lambda i: (0, 0))],
        out_specs=row,
        out_shape=jax.ShapeDtypeStruct((t, d), F32),
        compiler_params=_cparams(("arbitrary",)),
        name="merge_out",
    )(a_out, p_out, r_out, z_rest, z_rest, z_rest, x2, g1, w_out.astype(BF16))


def _peer_query_kernel(x_ref, g_ref, sc_ref, sh_ref, wq_ref, keys_ref, ht_ref, s_ref):
    hf = _rms(x_ref[...], g_ref[...]) * (1.0 + sc_ref[0]) + sh_ref[0]
    h = hf.astype(BF16)
    ht_ref[...] = hf.T.astype(BF16)
    dq = 2 * PEER_DHALF
    for hd in range(PEER_HEADS):
        q = _dot(h, wq_ref[:, hd * dq:(hd + 1) * dq]).astype(BF16)
        for p in range(2):
            s_ref[hd, p] = _dot_nt(keys_ref[p], q[:, p * PEER_DHALF:(p + 1) * PEER_DHALF])


def _peer_query(x2, g, sc, sh, w_query, sub_keys, seq):
    t, d = x2.shape
    tm = min(512, seq)
    nq = w_query.shape[1]
    return pl.pallas_call(
        _peer_query_kernel,
        grid=(t // tm,),
        in_specs=[pl.BlockSpec((tm, d), lambda i: (i, 0)),
                  pl.BlockSpec((1, d), lambda i: (0, 0)),
                  pl.BlockSpec((1, 1, d), lambda i: ((i * tm) // seq, 0, 0)),
                  pl.BlockSpec((1, 1, d), lambda i: ((i * tm) // seq, 0, 0)),
                  pl.BlockSpec((d, nq), lambda i: (0, 0)),
                  pl.BlockSpec((2, PEER_NKEYS, PEER_DHALF), lambda i: (0, 0, 0))],
        out_specs=[pl.BlockSpec((d, tm), lambda i: (0, i)),
                   pl.BlockSpec((PEER_HEADS, 2, PEER_NKEYS, tm), lambda i: (0, 0, 0, i))],
        out_shape=[jax.ShapeDtypeStruct((d, t), BF16),
                   jax.ShapeDtypeStruct((PEER_HEADS, 2, PEER_NKEYS, t), F32)],
        compiler_params=_cparams(("arbitrary",)),
        name="peer_query",
    )(x2, g.reshape(1, d), sc, sh, w_query.astype(BF16), sub_keys.astype(BF16))


NEG_BIG = -3.0e38


def _top_values(s, n):
    vals = []
    cur = s
    for k in range(n):
        m = jnp.max(cur, axis=0, keepdims=True)
        vals.append(m)
        if k + 1 < n:
            cur = jnp.where(cur == m, NEG_BIG, cur)
    return vals


_PEER_N = PEER_TOPK + 1
_PEER_PAIRS = [(i, j) for i in range(_PEER_N) for j in range(_PEER_N) if (i + 1) * (j + 1) <= _PEER_N]
_PEER_CAND_ROWS = -(-len(_PEER_PAIRS) // 8) * 8


def _peer_select_kernel(s_ref, c_ref, g_ref, eb_ref, cand_scr, wts_scr):
    n = _PEER_N
    tt = cand_scr.shape[1]
    npairs = len(_PEER_PAIRS)
    cand_scr[npairs:, :] = jnp.full((_PEER_CAND_ROWS - npairs, tt), NEG_BIG, F32)
    wts_scr[npairs:, :] = jnp.zeros((_PEER_CAND_ROWS - npairs, tt), F32)
    for hd in range(PEER_HEADS):
        sa = s_ref[hd, 0]
        sb = s_ref[hd, 1]
        ta = _top_values(sa, n)
        tb = _top_values(sb, n)
        ea_top = [jnp.exp(v - ta[0]) for v in ta]
        eb_top = [jnp.exp(v - tb[0]) for v in tb]
        for k, (i, j) in enumerate(_PEER_PAIRS):
            cand_scr[k:k + 1, :] = ta[i] + tb[j]
            wts_scr[k:k + 1, :] = ea_top[i] * eb_top[j]
        cand = cand_scr[...]
        top = _top_values(cand, n)
        tau = 0.5 * (top[PEER_TOPK - 1] + top[PEER_TOPK])
        zsum = jnp.sum(jnp.where(cand > tau, wts_scr[...], 0.0), axis=0, keepdims=True)
        c_ref[hd] = tau - sa
        g_ref[hd] = jnp.exp(sa - ta[0]) / (zsum * (2.0 ** 0.5))
        eb_ref[hd] = jnp.exp(sb - tb[0])


def _peer_select(scores):
    heads, _, nk, t = scores.shape
    tt = min(256, t)
    spec = pl.BlockSpec((heads, nk, tt), lambda i: (0, 0, i))
    shape = jax.ShapeDtypeStruct((heads, nk, t), F32)
    return pl.pallas_call(
        _peer_select_kernel,
        grid=(t // tt,),
        in_specs=[pl.BlockSpec((heads, 2, nk, tt), lambda i: (0, 0, 0, i))],
        out_specs=[spec, spec, spec],
        out_shape=[shape, shape, shape],
        scratch_shapes=[pltpu.VMEM((_PEER_CAND_ROWS, tt), F32), pltpu.VMEM((_PEER_CAND_ROWS, tt), F32)],
        compiler_params=_cparams(("arbitrary",)),
        name="peer_select",
    )(scores)


def _peer_dense_kernel(ht_ref, u_ref, vt_ref, sb_ref, eb_ref, c_ref, g_ref, x_ref, g2_ref, fg_ref,
                       o_ref, acc_scr, act_scr, p_scr, *, final_norm, rows_per_step, tt):
    e = pl.program_id(1)

    @pl.when(e == 0)
    def _():
        acc_scr[...] = jnp.zeros(acc_scr.shape, F32)

    jg = 32

    def tile_body(ls, j0):
        w = [jnp.zeros((jg, LANES), F32) for _ in range(rows_per_step)]
        for hd in range(PEER_HEADS):
            sbv = sb_ref[hd, pl.ds(j0, jg), ls]
            ebv = eb_ref[hd, pl.ds(j0, jg), ls]
            for r in range(rows_per_step):
                w[r] = w[r] + jnp.where(sbv > c_ref[hd, r:r + 1, ls], ebv * g_ref[hd, r:r + 1, ls], 0.0)
        for r in range(rows_per_step):
            rows = pl.ds(pl.multiple_of(r * PEER_NKEYS + j0, jg), jg)
            a = act_scr[rows, ls]
            p_scr[rows, ls] = ((w[r] * a) * (1.0 + lax.erf(a))).astype(BF16)

    act_scr[...] = _dot(u_ref[...], ht_ref[...])
    for lc in range(tt // LANES):
        ls = slice(lc * LANES, (lc + 1) * LANES)

        def loop_body(jt, carry, ls=ls):
            tile_body(ls, pl.multiple_of(jt * jg, jg))
            return carry

        lax.fori_loop(0, PEER_NKEYS // jg, loop_body, 0)
    acc_scr[...] += _dot(vt_ref[...], p_scr[...])

    @pl.when(e == pl.num_programs(1) - 1)
    def _():
        y = x_ref[...] + g2_ref[0] * acc_scr[...].T
        if final_norm:
            y = _rms(y, fg_ref[...])
        o_ref[...] = y


def _peer_dense(ht_bf, u_tab, v_tab, scores, c_arr, g_arr, eb_arr, x2, g2, final_g, final_norm, seq):
    t, d = x2.shape
    n_exp = u_tab.shape[0]
    heads, _, nk, _ = scores.shape
    tt = min(512, seq)
    rows_per_step = 8
    et = rows_per_step * nk
    vt_tiles = v_tab.astype(BF16).reshape(n_exp // et, et, d).transpose(0, 2, 1)
    kern = functools.partial(_peer_dense_kernel, final_norm=final_norm,
                             rows_per_step=rows_per_step, tt=tt)
    return pl.pallas_call(
        kern,
        grid=(t // tt, n_exp // et),
        in_specs=[pl.BlockSpec((d, tt), lambda i, e: (0, i)),
                  pl.BlockSpec((et, d), lambda i, e: (e, 0)),
                  pl.BlockSpec((None, d, et), lambda i, e: (e, 0, 0)),
                  pl.BlockSpec((heads, None, nk, tt), lambda i, e: (0, 1, 0, i)),
                  pl.BlockSpec((heads, nk, tt), lambda i, e: (0, 0, i)),
                  pl.BlockSpec((heads, rows_per_step, tt), lambda i, e: (0, e, i)),
                  pl.BlockSpec((heads, rows_per_step, tt), lambda i, e: (0, e, i)),
                  pl.BlockSpec((tt, d), lambda i, e: (i, 0)),
                  pl.BlockSpec((1, 1, d), lambda i, e: ((i * tt) // seq, 0, 0)),
                  pl.BlockSpec((1, d), lambda i, e: (0, 0))],
        out_specs=pl.BlockSpec((tt, d), lambda i, e: (i, 0)),
        out_shape=jax.ShapeDtypeStruct((t, d), F32),
        scratch_shapes=[pltpu.VMEM((d, tt), F32), pltpu.VMEM((et, tt), F32),
                        pltpu.VMEM((et, tt), BF16)],
        compiler_params=_cparams(("arbitrary", "arbitrary")),
        name="peer_dense",
    )(ht_bf, (u_tab * (2.0 ** -0.5)).astype(BF16), vt_tiles, scores, eb_arr, c_arr, g_arr, x2, g2,
      final_g.reshape(1, d))


def kernel(x, c, ada_w, ada_b, norm_mix_g, w_in, diff_lambda, diff_subln_g, pool_w, pool_scale,
           hgrn_lower_bounds, hgrn_norm_g, w_out, norm_ffn_g, peer_w_query, peer_sub_keys,
           peer_u, peer_v, final_g):
    bsz, seq, d = x.shape
    depth = ada_w.shape[0]
    att_cols = 3 * d
    x2 = x.reshape(bsz * seq, d)
    ada = _ada(c, ada_w, ada_b)
    for l in range(depth):
        sh1, sc1, g1, sh2, sc2, g2 = [ada[l, :, None, k * d:(k + 1) * d] for k in range(6)]
        w_l = w_in[l].astype(BF16)
        z_att = _norm_mod_matmul(x2, norm_mix_g[l], sc1, sh1, w_l[:, :att_cols], BF16, seq, head_major=True)
        z_rest = _norm_mod_matmul(x2, norm_mix_g[l], sc1, sh1, w_l[:, att_cols:], BF16, seq)
        z_f = _norm_mod_matmul(x2, norm_mix_g[l], sc1, sh1, w_l[:, att_cols + 2 * d:att_cols + 3 * d], F32, seq)
        a_out = _diff_attention(z_att, diff_lambda[l], diff_subln_g[l], l, bsz, seq)
        p_out = _multiscale_pool(z_rest, pool_w[l], pool_scale[l], bsz, seq)
        r_out = _hgrn2(z_rest, z_f, hgrn_lower_bounds, hgrn_norm_g[l], l, bsz, seq, d)
        x2 = _merge_out(a_out, p_out, r_out, z_rest, x2, g1, w_out[l], seq)
        ht_bf, scores = _peer_query(x2, norm_ffn_g[l], sc2, sh2, peer_w_query[l], peer_sub_keys[l], seq)
        c_arr, g_arr, eb_arr = _peer_select(scores)
        x2 = _peer_dense(ht_bf, peer_u[l], peer_v[l], scores, c_arr, g_arr, eb_arr, x2, g2,
                         final_g, l == depth - 1, seq)
    return x2.reshape(bsz, seq, d)
```

```python
import functools
import math

import jax
import jax.numpy as jnp
from jax import lax
from jax.experimental import pallas as pl
from jax.experimental.pallas import tpu as pltpu

F32 = jnp.float32
BF16 = jnp.bfloat16

RMS_EPS = 1e-6
MASK_VALUE = -1e30
LB_FLOOR = 1e-20
LANES = 128

ATT_DK = 64
ATT_DV = 2 * ATT_DK
POOL_WINDOWS = (2, 4, 8, 16)
HGRN_D = 128
HGRN_CHUNK = 64
HGRN_SUB = 16
PEER_HEADS = 8
PEER_NKEYS = 128
PEER_TOPK = 16
PEER_DHALF = 128

VMEM_LIMIT = 56 * 1024 * 1024


def _cparams(sem):
    return pltpu.CompilerParams(dimension_semantics=sem, vmem_limit_bytes=VMEM_LIMIT)


def _dot(a, b, **kw):
    return jnp.dot(a, b, preferred_element_type=F32, **kw)


def _dot_nt(a, b):
    return lax.dot_general(a, b, (((1,), (1,)), ((), ())), preferred_element_type=F32)


def _dot_tn(a, b):
    return lax.dot_general(a, b, (((0,), (0,)), ((), ())), preferred_element_type=F32)


def _rms(x, g):
    return x * lax.rsqrt(jnp.mean(x * x, axis=-1, keepdims=True) + RMS_EPS) * g


def _silu(x):
    return x * jax.nn.sigmoid(x)


def _ada_kernel(c_ref, w_ref, b_ref, o_ref):
    o_ref[0] = _dot(_silu(c_ref[...]), w_ref[0], precision=lax.Precision.HIGHEST) + b_ref[0]


def _ada(c, ada_w, ada_b):
    depth, d, n = ada_w.shape
    bsz = c.shape[0]
    tn = 1024
    return pl.pallas_call(
        _ada_kernel,
        grid=(depth, n // tn),
        in_specs=[pl.BlockSpec((bsz, d), lambda l, j: (0, 0)),
                  pl.BlockSpec((1, d, tn), lambda l, j: (l, 0, j)),
                  pl.BlockSpec((1, 1, tn), lambda l, j: (l, 0, j))],
        out_specs=pl.BlockSpec((1, bsz, tn), lambda l, j: (l, 0, j)),
        out_shape=jax.ShapeDtypeStruct((depth, bsz, n), F32),
        compiler_params=_cparams(("arbitrary", "arbitrary")),
        name="ada_proj",
    )(c, ada_w, ada_b.reshape(depth, 1, n))


def _norm_mod_matmul_kernel(x_ref, g_ref, sc_ref, sh_ref, w_ref, o_ref, h_scr):
    @pl.when(pl.program_id(1) == 0)
    def _():
        h = _rms(x_ref[...], g_ref[...]) * (1.0 + sc_ref[0]) + sh_ref[0]
        h_scr[...] = h.astype(BF16)

    res = _dot(h_scr[...], w_ref[...]).astype(o_ref.dtype)
    if len(o_ref.shape) == 2:
        o_ref[...] = res
    else:
        for hh in range(o_ref.shape[0]):
            o_ref[hh] = res[:, hh * LANES:(hh + 1) * LANES]


def _norm_mod_matmul(x2, g, sc, sh, w, out_dtype, seq, head_major=False):
    t, d = x2.shape
    n = w.shape[1]
    tm = min(2048, seq)
    tn = 1024
    if head_major:
        out_spec = pl.BlockSpec((tn // LANES, tm, LANES), lambda i, j: (j, i, 0))
        out_shape = jax.ShapeDtypeStruct((n // LANES, t, LANES), out_dtype)
    else:
        out_spec = pl.BlockSpec((tm, tn), lambda i, j: (i, j))
        out_shape = jax.ShapeDtypeStruct((t, n), out_dtype)
    return pl.pallas_call(
        _norm_mod_matmul_kernel,
        grid=(t // tm, n // tn),
        in_specs=[pl.BlockSpec((tm, d), lambda i, j: (i, 0)),
                  pl.BlockSpec((1, d), lambda i, j: (0, 0)),
                  pl.BlockSpec((1, 1, d), lambda i, j: ((i * tm) // seq, 0, 0)),
                  pl.BlockSpec((1, 1, d), lambda i, j: ((i * tm) // seq, 0, 0)),
                  pl.BlockSpec((d, tn), lambda i, j: (0, j))],
        out_specs=out_spec,
        out_shape=out_shape,
        scratch_shapes=[pltpu.VMEM((tm, d), BF16)],
        compiler_params=_cparams(("arbitrary", "arbitrary")),
        name="norm_mod_matmul",
    )(x2, g.reshape(1, d), sc, sh, w)


LOG2E = 1.4426950408889634


def _attn_kernel(lam_ref, g_ref, slope_ref, q_ref, k_ref, v_ref, o_ref, s_scr, mx_scr, ls_scr, acc_scr,
                 *, lam_init, tq):
    qi = pl.program_id(2)
    slope = slope_ref[0] * LOG2E
    lane = lax.broadcasted_iota(jnp.int32, (1, 2 * ATT_DK), 1)
    q = (q_ref[...].astype(F32) * (ATT_DK ** -0.5 * LOG2E)).astype(BF16)
    zero = jnp.zeros_like(q)
    qq = jnp.concatenate([jnp.where(lane < ATT_DK, q, zero), jnp.where(lane >= ATT_DK, q, zero)], axis=0)
    col = lax.broadcasted_iota(jnp.int32, (1, tq), 1)

    def logits(j):
        k = k_ref[pl.ds(pl.multiple_of(j * tq, tq), tq), :]
        bias = slope * (col + (j - qi) * tq).astype(F32)
        return _dot_nt(qq, k) + bias

    def fold_max(s):
        m = s[:, :LANES]
        for c in range(1, tq // LANES):
            m = jnp.maximum(m, s[:, c * LANES:(c + 1) * LANES])
        mx_scr[...] = jnp.maximum(mx_scr[...], m)

    mx_scr[...] = jnp.full(mx_scr.shape, MASK_VALUE, F32)

    def pass_a(j, carry):
        s = logits(j)
        s_scr[j] = s
        fold_max(s)
        return carry

    lax.fori_loop(0, qi, pass_a, 0)
    row = lax.broadcasted_iota(jnp.int32, (2 * tq, tq), 0) & (tq - 1)
    keep = row >= lax.broadcasted_iota(jnp.int32, (2 * tq, tq), 1)
    s = jnp.where(keep, logits(qi), MASK_VALUE)
    s_scr[qi] = s
    fold_max(s)

    m_row = jnp.max(mx_scr[...], axis=1, keepdims=True)
    mx_scr[...] = jnp.broadcast_to(m_row, mx_scr.shape)
    ls_scr[...] = jnp.zeros(ls_scr.shape, F32)
    acc_scr[...] = jnp.zeros(acc_scr.shape, F32)

    def pass_b(j, carry):
        s = s_scr[j]
        m = mx_scr[...]
        ps = [jnp.exp2(s[:, c * LANES:(c + 1) * LANES] - m) for c in range(tq // LANES)]
        tot = ps[0]
        for p in ps[1:]:
            tot = tot + p
        ls_scr[...] += tot
        p = jnp.concatenate(ps, axis=1).astype(BF16)
        v = v_ref[pl.ds(pl.multiple_of(j * tq, tq), tq), :]
        acc_scr[...] += _dot(p, v)
        return carry

    lax.fori_loop(0, qi + 1, pass_b, 0)

    lq = lam_ref[...]
    lam = (jnp.exp(jnp.sum(lq[0:1] * lq[1:2], axis=1, keepdims=True))
           - jnp.exp(jnp.sum(lq[2:3] * lq[3:4], axis=1, keepdims=True)) + lam_init)
    o2 = acc_scr[...] / jnp.sum(ls_scr[...], axis=1, keepdims=True)
    o = o2[:tq] - lam * o2[tq:]
    o_ref[...] = (_rms(o, g_ref[...]) * (1.0 - lam_init)).astype(o_ref.dtype)


def _diff_attention(z_att, lam_qk, subln_g, layer, bsz, seq):
    t = z_att.shape[1]
    heads = z_att.shape[0] // 3
    tq = min(512, seq)
    nq = seq // tq
    lam_init = 0.8 - 0.6 * math.exp(-0.3 * layer)
    slopes = jnp.asarray([2.0 ** (-8.0 * (h + 1) / heads) for h in range(heads)], F32)
    slopes = jnp.broadcast_to(slopes[:, None, None], (heads, 1, tq))
    kern = functools.partial(_attn_kernel, lam_init=lam_init, tq=tq)
    return pl.pallas_call(
        kern,
        grid=(bsz, heads, nq),
        in_specs=[pl.BlockSpec((4, ATT_DK), lambda b, h, i: (0, 0)),
                  pl.BlockSpec((1, ATT_DV), lambda b, h, i: (0, 0)),
                  pl.BlockSpec((1, 1, tq), lambda b, h, i: (h, 0, 0)),
                  pl.BlockSpec((None, tq, ATT_DV), lambda b, h, i: (h, b * nq + i, 0)),
                  pl.BlockSpec((None, seq, ATT_DV), lambda b, h, i: (heads + h, b, 0)),
                  pl.BlockSpec((None, seq, ATT_DV), lambda b, h, i: (2 * heads + h, b, 0))],
        out_specs=pl.BlockSpec((None, tq, ATT_DV), lambda b, h, i: (h, b * nq + i, 0)),
        out_shape=jax.ShapeDtypeStruct((heads, t, ATT_DV), BF16),
        scratch_shapes=[pltpu.VMEM((nq, 2 * tq, tq), F32), pltpu.VMEM((2 * tq, LANES), F32),
                        pltpu.VMEM((2 * tq, LANES), F32), pltpu.VMEM((2 * tq, ATT_DV), F32)],
        compiler_params=_cparams(("arbitrary", "arbitrary", "arbitrary")),
        name="diff_attention",
    )(lam_qk, subln_g.reshape(1, ATT_DV), slopes, z_att, z_att, z_att)


def _pool_kernel(p_ref, w_ref, sc_ref, o_ref, pad_scr, *, seq):
    g = pl.program_id(1)
    pad = max(POOL_WINDOWS)
    p = p_ref[...].astype(F32)
    pad_scr[0:pad, :] = jnp.zeros((pad, p.shape[1]), F32)
    pad_scr[pad:pad + seq, :] = p
    t1 = lax.broadcasted_iota(jnp.int32, (seq, 1), 0) + 1

    for gi, win in enumerate(POOL_WINDOWS):
        @pl.when(g == gi)
        def _(win=win):
            acc = p
            for s in range(1, win):
                acc = acc + pad_scr[pad - s:pad - s + seq, :]
            count = jnp.minimum(t1, win).astype(F32)
            pooled = acc / count - p
            y = _dot(pooled.astype(BF16), w_ref[0])
            o_ref[...] = (y * sc_ref[0]).astype(o_ref.dtype)


def _multiscale_pool(z_rest, pool_w, pool_scale, bsz, seq):
    t = z_rest.shape[0]
    groups, gw, _ = pool_w.shape
    kern = functools.partial(_pool_kernel, seq=seq)
    return pl.pallas_call(
        kern,
        grid=(bsz, groups),
        in_specs=[pl.BlockSpec((seq, gw), lambda b, g: (b, g)),
                  pl.BlockSpec((1, gw, gw), lambda b, g: (g, 0, 0)),
                  pl.BlockSpec((1, 1, gw), lambda b, g: (g, 0, 0))],
        out_specs=pl.BlockSpec((seq, gw), lambda b, g: (b, g)),
        out_shape=jax.ShapeDtypeStruct((t, groups * gw), BF16),
        scratch_shapes=[pltpu.VMEM((seq + max(POOL_WINDOWS), gw), F32)],
        compiler_params=_cparams(("arbitrary", "arbitrary")),
        name="multiscale_pool",
    )(z_rest, pool_w.astype(BF16), pool_scale.reshape(groups, 1, gw))


def _log_sigmoid(z):
    return jnp.minimum(z, 0.0) - jnp.log1p(jnp.exp(-jnp.abs(z)))


def _logaddexp(a, b):
    return jnp.maximum(a, b) + jnp.log1p(jnp.exp(-jnp.abs(a - b)))


def _hgrn_kernel(lbp_ref, ng_ref, q_ref, f_ref, i_ref, g_ref, o_ref, st_scr, b_scr, k_scr, v_scr,
                 *, layer, seq, hb):
    chunk, sub = HGRN_CHUNK, HGRN_SUB
    lbp = lbp_ref[...]
    e = jnp.exp(lbp - jnp.max(lbp, axis=0, keepdims=True))
    prob = e / jnp.sum(e, axis=0, keepdims=True)
    cs = prob[0:1]
    for l in range(1, layer + 1):
        cs = cs + prob[l:l + 1]
    lb = cs - prob[0:1]
    log_lb = jnp.log(jnp.maximum(lb, LB_FLOOR))
    log_1m = jnp.log1p(-lb)

    tril = (lax.broadcasted_iota(jnp.int32, (chunk, chunk), 0)
            >= lax.broadcasted_iota(jnp.int32, (chunk, chunk), 1)).astype(F32)
    trow = lax.broadcasted_iota(jnp.int32, (sub, 1), 0)

    @pl.when(pl.program_id(2) == 0)
    def _():
        st_scr[...] = jnp.zeros(st_scr.shape, F32)

    ng = ng_ref[...]

    def head_chunk(hh, rows):
        hs = slice(hh * HGRN_D, (hh + 1) * HGRN_D)
        logf = _logaddexp(log_lb[:, hs], log_1m[:, hs] + _log_sigmoid(f_ref[rows, hs]))
        b = _dot(tril, logf, precision=lax.Precision.HIGHEST)
        kk = 1.0 - jnp.exp(logf)
        qs = _silu(q_ref[rows, hs].astype(F32))
        v = i_ref[rows, hs].astype(F32)
        b_scr[hh] = b
        k_scr[hh] = kk
        v_scr[hh] = v
        st = st_scr[hh]
        vb = v.astype(BF16)
        o_inter = _dot_nt((qs * jnp.exp(b)).astype(BF16), st.astype(BF16))

        outs = []
        for blk in range(chunk // sub):
            s0 = blk * sub
            b_i = b[s0:s0 + sub]
            q_i = qs[s0:s0 + sub]
            o_i = o_inter[s0:s0 + sub]
            if blk > 0:
                b_ref0 = b[s0:s0 + 1]
                k_prev = kk[:s0] * jnp.exp(b_ref0 - b[:s0])
                a = _dot_nt((q_i * jnp.exp(b_i - b_ref0)).astype(BF16), k_prev.astype(BF16))
                o_i = o_i + _dot(a.astype(BF16), vb[:s0])
            for s in range(sub):
                b_s = b_scr[hh, s0 + s:s0 + s + 1, :]
                k_s = k_scr[hh, s0 + s:s0 + s + 1, :]
                v_s = v_scr[hh, s0 + s:s0 + s + 1, :]
                dec = jnp.exp(jnp.where(trow >= s, b_i - b_s, MASK_VALUE))
                colv = jnp.sum(q_i * dec * k_s, axis=1, keepdims=True)
                o_i = o_i + colv * v_s
            outs.append(o_i)
        o = jnp.concatenate(outs, axis=0)
        o_ref[rows, hs] = (_rms(o, ng) * _silu(g_ref[rows, hs].astype(F32))).astype(o_ref.dtype)

        b_last = b[chunk - 1:chunk]
        k_dec = kk * jnp.exp(b_last - b)
        st_scr[hh] = jnp.exp(b_last) * st + _dot_tn(vb, k_dec.astype(BF16))

    def chunk_body(c, carry):
        rows = pl.ds(pl.multiple_of(c * chunk, chunk), chunk)
        for hh in range(hb):
            head_chunk(hh, rows)
        return carry

    lax.fori_loop(0, seq // chunk, chunk_body, 0)


def _hgrn2(z_rest, z_f, lower_bounds, norm_g, layer, bsz, seq, d_model):
    t = z_rest.shape[0]
    heads = d_model // HGRN_D
    depth = lower_bounds.shape[0]
    hb = 8
    hw = hb * HGRN_D
    ng = heads // hb
    ts = min(512, seq)
    ns = seq // ts
    kern = functools.partial(_hgrn_kernel, layer=layer, seq=ts, hb=hb)

    def col(k):
        return pl.BlockSpec((ts, hw), lambda b, h, s, k=k: (b * ns + s, k * ng + h))

    return pl.pallas_call(
        kern,
        grid=(bsz, ng, ns),
        in_specs=[pl.BlockSpec((depth, hw), lambda b, h, s: (0, h)),
                  pl.BlockSpec((1, HGRN_D), lambda b, h, s: (0, 0)),
                  col(1), pl.BlockSpec((ts, hw), lambda b, h, s: (b * ns + s, h)), col(3), col(4)],
        out_specs=pl.BlockSpec((ts, hw), lambda b, h, s: (b * ns + s, h)),
        out_shape=jax.ShapeDtypeStruct((t, d_model), BF16),
        scratch_shapes=[pltpu.VMEM((hb, HGRN_D, HGRN_D), F32),
                        pltpu.VMEM((hb, HGRN_CHUNK, HGRN_D), F32),
                        pltpu.VMEM((hb, HGRN_CHUNK, HGRN_D), F32),
                        pltpu.VMEM((hb, HGRN_CHUNK, HGRN_D), F32)],
        compiler_params=_cparams(("arbitrary", "arbitrary", "arbitrary")),
        name="hgrn2",
    )(lower_bounds, norm_g.reshape(1, HGRN_D), z_rest, z_f, z_rest, z_rest)


def _merge_kernel(a_ref, p_ref, r_ref, ga_ref, gp_ref, gr_ref, x_ref, g1_ref, w_ref, o_ref):
    a = jnp.concatenate([a_ref[hh] for hh in range(a_ref.shape[0])], axis=1)
    merged = (jax.nn.sigmoid(ga_ref[...].astype(F32)) * a + jax.nn.sigmoid(gp_ref[...].astype(F32)) * p_ref[...]
              + jax.nn.sigmoid(gr_ref[...].astype(F32)) * r_ref[...])
    o_ref[...] = x_ref[...] + g1_ref[0] * _dot(merged.astype(BF16), w_ref[...])


def _merge_out(a_out, p_out, r_out, z_rest, x2, g1, w_out, seq):
    t, d = x2.shape
    tm = min(256, seq)
    row = pl.BlockSpec((tm, d), lambda i: (i, 0))

    def gate(k):
        return pl.BlockSpec((tm, d), lambda i, k=k: (i, 5 + k))

    return pl.pallas_call(
        _merge_kernel,
        grid=(t // tm,),
        in_specs=[pl.BlockSpec((d // ATT_DV, tm, ATT_DV), lambda i: (0, i, 0)), row, row,
                  gate(0), gate(1), gate(2), row,
                  pl.BlockSpec((1, 1, d), lambda i: ((i * tm) // seq, 0, 0)),
                  pl.BlockSpec((d, d), lambda i: (0, 0))],
        out_specs=row,
        out_shape=jax.ShapeDtypeStruct((t, d), F32),
        compiler_params=_cparams(("arbitrary",)),
        name="merge_out",
    )(a_out, p_out, r_out, z_rest, z_rest, z_rest, x2, g1, w_out.astype(BF16))


def _peer_query_kernel(x_ref, g_ref, sc_ref, sh_ref, wq_ref, keys_ref, ht_ref, s_ref):
    hf = _rms(x_ref[...], g_ref[...]) * (1.0 + sc_ref[0]) + sh_ref[0]
    h = hf.astype(BF16)
    ht_ref[...] = hf.T.astype(BF16)
    dq = 2 * PEER_DHALF
    for hd in range(PEER_HEADS):
        q = _dot(h, wq_ref[:, hd * dq:(hd + 1) * dq]).astype(BF16)
        for p in range(2):
            s_ref[hd, p] = _dot_nt(keys_ref[p], q[:, p * PEER_DHALF:(p + 1) * PEER_DHALF])


def _peer_query(x2, g, sc, sh, w_query, sub_keys, seq):
    t, d = x2.shape
    tm = min(512, seq)
    nq = w_query.shape[1]
    return pl.pallas_call(
        _peer_query_kernel,
        grid=(t // tm,),
        in_specs=[pl.BlockSpec((tm, d), lambda i: (i, 0)),
                  pl.BlockSpec((1, d), lambda i: (0, 0)),
                  pl.BlockSpec((1, 1, d), lambda i: ((i * tm) // seq, 0, 0)),
                  pl.BlockSpec((1, 1, d), lambda i: ((i * tm) // seq, 0, 0)),
                  pl.BlockSpec((d, nq), lambda i: (0, 0)),
                  pl.BlockSpec((2, PEER_NKEYS, PEER_DHALF), lambda i: (0, 0, 0))],
        out_specs=[pl.BlockSpec((d, tm), lambda i: (0, i)),
                   pl.BlockSpec((PEER_HEADS, 2, PEER_NKEYS, tm), lambda i: (0, 0, 0, i))],
        out_shape=[jax.ShapeDtypeStruct((d, t), BF16),
                   jax.ShapeDtypeStruct((PEER_HEADS, 2, PEER_NKEYS, t), F32)],
        compiler_params=_cparams(("arbitrary",)),
        name="peer_query",
    )(x2, g.reshape(1, d), sc, sh, w_query.astype(BF16), sub_keys.astype(BF16))


NEG_BIG = -3.0e38


def _top_values(s, n):
    vals = []
    cur = s
    for k in range(n):
        m = jnp.max(cur, axis=0, keepdims=True)
        vals.append(m)
        if k + 1 < n:
            cur = jnp.where(cur == m, NEG_BIG, cur)
    return vals


_PEER_N = PEER_TOPK + 1
_PEER_PAIRS = [(i, j) for i in range(_PEER_N) for j in range(_PEER_N) if (i + 1) * (j + 1) <= _PEER_N]
_PEER_CAND_ROWS = -(-len(_PEER_PAIRS) // 8) * 8


def _peer_select_kernel(s_ref, c_ref, g_ref, eb_ref, cand_scr, wts_scr):
    n = _PEER_N
    tt = cand_scr.shape[1]
    npairs = len(_PEER_PAIRS)
    cand_scr[npairs:, :] = jnp.full((_PEER_CAND_ROWS - npairs, tt), NEG_BIG, F32)
    wts_scr[npairs:, :] = jnp.zeros((_PEER_CAND_ROWS - npairs, tt), F32)
    for hd in range(PEER_HEADS):
        sa = s_ref[hd, 0]
        sb = s_ref[hd, 1]
        ta = _top_values(sa, n)
        tb = _top_values(sb, n)
        ea_top = [jnp.exp(v - ta[0]) for v in ta]
        eb_top = [jnp.exp(v - tb[0]) for v in tb]
        for k, (i, j) in enumerate(_PEER_PAIRS):
            cand_scr[k:k + 1, :] = ta[i] + tb[j]
            wts_scr[k:k + 1, :] = ea_top[i] * eb_top[j]
        cand = cand_scr[...]
        top = _top_values(cand, n)
        tau = 0.5 * (top[PEER_TOPK - 1] + top[PEER_TOPK])
        zsum = jnp.sum(jnp.where(cand > tau, wts_scr[...], 0.0), axis=0, keepdims=True)
        c_ref[hd] = tau - sa
        g_ref[hd] = jnp.exp(sa - ta[0]) / (zsum * (2.0 ** 0.5))
        eb_ref[hd] = jnp.exp(sb - tb[0])


def _peer_select(scores):
    heads, _, nk, t = scores.shape
    tt = min(256, t)
    spec = pl.BlockSpec((heads, nk, tt), lambda i: (0, 0, i))
    shape = jax.ShapeDtypeStruct((heads, nk, t), F32)
    return pl.pallas_call(
        _peer_select_kernel,
        grid=(t // tt,),
        in_specs=[pl.BlockSpec((heads, 2, nk, tt), lambda i: (0, 0, 0, i))],
        out_specs=[spec, spec, spec],
        out_shape=[shape, shape, shape],
        scratch_shapes=[pltpu.VMEM((_PEER_CAND_ROWS, tt), F32), pltpu.VMEM((_PEER_CAND_ROWS, tt), F32)],
        compiler_params=_cparams(("arbitrary",)),
        name="peer_select",
    )(scores)


def _peer_dense_kernel(ht_ref, u_ref, vt_ref, sb_ref, eb_ref, c_ref, g_ref, x_ref, g2_ref, fg_ref,
                       o_ref, acc_scr, act_scr, p_scr, *, final_norm, rows_per_step, tt):
    e = pl.program_id(1)

    @pl.when(e == 0)
    def _():
        acc_scr[...] = jnp.zeros(acc_scr.shape, F32)

    jg = 32

    def tile_body(ls, j0):
        w = [jnp.zeros((jg, LANES), F32) for _ in range(rows_per_step)]
        for hd in range(PEER_HEADS):
            sbv = sb_ref[hd, pl.ds(j0, jg), ls]
            ebv = eb_ref[hd, pl.ds(j0, jg), ls]
            for r in range(rows_per_step):
                w[r] = w[r] + jnp.where(sbv > c_ref[hd, r:r + 1, ls], ebv * g_ref[hd, r:r + 1, ls], 0.0)
        for r in range(rows_per_step):
            rows = pl.ds(pl.multiple_of(r * PEER_NKEYS + j0, jg), jg)
            a = act_scr[rows, ls]
            p_scr[rows, ls] = ((w[r] * a) * (1.0 + lax.erf(a))).astype(BF16)

    act_scr[...] = _dot(u_ref[...], ht_ref[...])
    for lc in range(tt // LANES):
        ls = slice(lc * LANES, (lc + 1) * LANES)

        def loop_body(jt, carry, ls=ls):
            tile_body(ls, pl.multiple_of(jt * jg, jg))
            return carry

        lax.fori_loop(0, PEER_NKEYS // jg, loop_body, 0)
    acc_scr[...] += _dot(vt_ref[...], p_scr[...])

    @pl.when(e == pl.num_programs(1) - 1)
    def _():
        y = x_ref[...] + g2_ref[0] * acc_scr[...].T
        if final_norm:
            y = _rms(y, fg_ref[...])
        o_ref[...] = y


def _peer_dense(ht_bf, u_tab, v_tab, scores, c_arr, g_arr, eb_arr, x2, g2, final_g, final_norm, seq):
    t, d = x2.shape
    n_exp = u_tab.shape[0]
    heads, _, nk, _ = scores.shape
    tt = min(512, seq)
    rows_per_step = 8
    et = rows_per_step * nk
    vt_tiles = v_tab.astype(BF16).reshape(n_exp // et, et, d).transpose(0, 2, 1)
    kern = functools.partial(_peer_dense_kernel, final_norm=final_norm,
                             rows_per_step=rows_per_step, tt=tt)
    return pl.pallas_call(
        kern,
        grid=(t // tt, n_exp // et),
        in_specs=[pl.BlockSpec((d, tt), lambda i, e: (0, i)),
                  pl.BlockSpec((et, d), lambda i, e: (e, 0)),
                  pl.BlockSpec((None, d, et), lambda i, e: (e, 0, 0)),
                  pl.BlockSpec((heads, None, nk, tt), lambda i, e: (0, 1, 0, i)),
                  pl.BlockSpec((heads, nk, tt), lambda i, e: (0, 0, i)),
                  pl.BlockSpec((heads, rows_per_step, tt), lambda i, e: (0, e, i)),
                  pl.BlockSpec((heads, rows_per_step, tt), lambda i, e: (0, e, i)),
                  pl.BlockSpec((tt, d), lambda i, e: (i, 0)),
                  pl.BlockSpec((1, 1, d), lambda i, e: ((i * tt) // seq, 0, 0)),
                  pl.BlockSpec((1, d), lambda i, e: (0, 0))],
        out_specs=pl.BlockSpec((tt, d), lambda i, e: (i, 0)),
        out_shape=jax.ShapeDtypeStruct((t, d), F32),
        scratch_shapes=[pltpu.VMEM((d, tt), F32), pltpu.VMEM((et, tt), F32),
                        pltpu.VMEM((et, tt), BF16)],
        compiler_params=_cparams(("arbitrary", "arbitrary")),
        name="peer_dense",
    )(ht_bf, (u_tab * (2.0 ** -0.5)).astype(BF16), vt_tiles, scores, eb_arr, c_arr, g_arr, x2, g2,
      final_g.reshape(1, d))


def kernel(x, c, ada_w, ada_b, norm_mix_g, w_in, diff_lambda, diff_subln_g, pool_w, pool_scale,
           hgrn_lower_bounds, hgrn_norm_g, w_out, norm_ffn_g, peer_w_query, peer_sub_keys,
           peer_u, peer_v, final_g):
    bsz, seq, d = x.shape
    depth = ada_w.shape[0]
    att_cols = 3 * d
    x2 = x.reshape(bsz * seq, d)
    ada = _ada(c, ada_w, ada_b)
    for l in range(depth):
        sh1, sc1, g1, sh2, sc2, g2 = [ada[l, :, None, k * d:(k + 1) * d] for k in range(6)]
        w_l = w_in[l].astype(BF16)
        z_att = _norm_mod_matmul(x2, norm_mix_g[l], sc1, sh1, w_l[:, :att_cols], BF16, seq, head_major=True)
        z_rest = _norm_mod_matmul(x2, norm_mix_g[l], sc1, sh1, w_l[:, att_cols:], BF16, seq)
        z_f = _norm_mod_matmul(x2, norm_mix_g[l], sc1, sh1, w_l[:, att_cols + 2 * d:att_cols + 3 * d], F32, seq)
        a_out = _diff_attention(z_att, diff_lambda[l], diff_subln_g[l], l, bsz, seq)
        p_out = _multiscale_pool(z_rest, pool_w[l], pool_scale[l], bsz, seq)
        r_out = _hgrn2(z_rest, z_f, hgrn_lower_bounds, hgrn_norm_g[l], l, bsz, seq, d)
        x2 = _merge_out(a_out, p_out, r_out, z_rest, x2, g1, w_out[l], seq)
        ht_bf, scores = _peer_query(x2, norm_ffn_g[l], sc2, sh2, peer_w_query[l], peer_sub_keys[l], seq)
        c_arr, g_arr, eb_arr = _peer_select(scores)
        x2 = _peer_dense(ht_bf, peer_u[l], peer_v[l], scores, c_arr, g_arr, eb_arr, x2, g2,
                         final_g, l == depth - 1, seq)
    return x2.reshape(bsz, seq, d)
```
